```python
import math
import jax
import jax.numpy as jnp
from jax import lax
import numpy as np


D_MODEL = 2048
BATCH = 8
SEQ = 4096
DEPTH = 1

RWKV_WIDTH = D_MODEL // 2
RWKV_HEAD_DIM = 64
RWKV_HEADS = RWKV_WIDTH // RWKV_HEAD_DIM
DECAY_LORA = max(32, int(round(1.8 * RWKV_WIDTH ** 0.5 / 32)) * 32)
AAA_LORA = max(32, int(round(1.8 * RWKV_WIDTH ** 0.5 / 32)) * 32)
GATE_LORA = max(32, int(round(0.6 * RWKV_WIDTH ** 0.8 / 32)) * 32)
RWKV_COLS = 3 * RWKV_WIDTH + DECAY_LORA + AAA_LORA + GATE_LORA
RWKV_SPLITS = (RWKV_WIDTH, 2 * RWKV_WIDTH, 3 * RWKV_WIDTH,
               3 * RWKV_WIDTH + DECAY_LORA, 3 * RWKV_WIDTH + DECAY_LORA + AAA_LORA)
GN_EPS = 64e-5
L2_EPS = 1e-12

POOL_WIDTH = D_MODEL // 2
POOL_WINDOWS = (2, 4, 8, 16)
POOL_GROUPS = len(POOL_WINDOWS)
POOL_GROUP_DIM = POOL_WIDTH // POOL_GROUPS

IN_COLS = RWKV_COLS + POOL_WIDTH + 2 * D_MODEL
IN_SPLITS = (RWKV_COLS, RWKV_COLS + POOL_WIDTH, RWKV_COLS + POOL_WIDTH + D_MODEL)

D_FF = int(math.ceil(8 * D_MODEL / 3 / 256)) * 256
MACARON_WEIGHT = 0.5
NORM_EPS = 1e-6

kernel_name = 'rwkv7_pool_macaron_hybrid'


def _rmsnorm(x, g):
    xf = x.astype(jnp.float32)
    y = xf * lax.rsqrt(jnp.mean(xf * xf, axis=-1, keepdims=True) + NORM_EPS)
    return (y * g.astype(jnp.float32)).astype(x.dtype)


def _swiglu(x, w_gate, w_up, w_down):
    return (jax.nn.silu(x @ w_gate) * (x @ w_up)) @ w_down


def _token_shift(z):
    return jnp.pad(z, ((0, 0), (1, 0), (0, 0)))[:, :-1]


def _wkv7_scan(r, w, k, v, a, b):
    bsz, _, nh, nd = r.shape
    xs = tuple(jnp.moveaxis(t, 1, 0) for t in (r, w, k, v, a, b))

    def step(state, inp):
        r_t, w_t, k_t, v_t, a_t, b_t = inp
        sa = jnp.einsum('bhvk,bhk->bhv', state, a_t)
        state = (state * w_t[:, :, None, :]
                 + sa[..., None] * b_t[:, :, None, :]
                 + v_t[..., None] * k_t[:, :, None, :])
        y_t = jnp.einsum('bhvk,bhk->bhv', state, r_t)
        return state, y_t

    s0 = jnp.zeros((bsz, nh, nd, nd), jnp.float32)
    _, ys = lax.scan(step, s0, xs)
    return jnp.moveaxis(ys, 0, 1)


def _rwkv7_branch(z, mu, w0, w2, a0, a2, g2, k_k, k_a, r_k, gn_w, gn_b):
    bsz, seq, _ = z.shape
    f32 = jnp.float32
    zf = z.astype(f32)
    zs = zf + (_token_shift(zf) - zf) * mu.astype(f32)
    r, k, v, lw, la, lg = jnp.split(zs, RWKV_SPLITS, axis=-1)
    w = -jax.nn.softplus(-(w0.astype(f32) + jnp.tanh(lw) @ w2.astype(f32))) - 0.5
    decay = jnp.exp(-jnp.exp(w))
    a = jax.nn.sigmoid(a0.astype(f32) + la @ a2.astype(f32))
    g = jax.nn.sigmoid(lg) @ g2.astype(f32)
    hv = lambda t: t.reshape(bsz, seq, RWKV_HEADS, RWKV_HEAD_DIM)
    kk = hv(k * k_k.astype(f32))
    kk = kk / jnp.maximum(jnp.sqrt(jnp.sum(kk * kk, axis=-1, keepdims=True)), L2_EPS)
    a_h = hv(a)
    k = k * (1.0 + (a - 1.0) * k_a.astype(f32))
    r_h, k_h, v_h = hv(r), hv(k), hv(v)
    y = _wkv7_scan(r_h, hv(decay), k_h, v_h, -kk, kk * a_h)
    mean = jnp.mean(y, axis=-1, keepdims=True)
    var = jnp.mean(jnp.square(y - mean), axis=-1, keepdims=True)
    y = (y - mean) * lax.rsqrt(var + GN_EPS)
    y = y.reshape(bsz, seq, RWKV_WIDTH) * gn_w.astype(f32) + gn_b.astype(f32)
    bonus = jnp.sum(r_h * k_h * r_k.astype(f32), axis=-1, keepdims=True) * v_h
    y = y + bonus.reshape(bsz, seq, RWKV_WIDTH)
    return (y * g).astype(z.dtype)


def _pool_branch(z, pool_w, pool_scale):
    bsz, seq, _ = z.shape
    zf = z.astype(jnp.float32).reshape(bsz, seq, POOL_GROUPS, POOL_GROUP_DIM)
    c0 = jnp.pad(jnp.cumsum(zf, axis=1), ((0, 0), (1, 0), (0, 0), (0, 0)))
    t = jnp.arange(1, seq + 1, dtype=jnp.float32)
    outs = []
    for gi, win in enumerate(POOL_WINDOWS):
        cg = c0[:, :, gi]
        upper = cg[:, 1:]
        lower = jnp.pad(cg[:, :seq - win + 1], ((0, 0), (win - 1, 0), (0, 0)))
        cnt = jnp.minimum(t, float(win))[None, :, None]
        outs.append((upper - lower) / cnt)
    pooled = jnp.stack(outs, axis=2)
    mixed = pooled - zf
    y = jnp.einsum('bsgc,gcd->bsgd', mixed, pool_w.astype(jnp.float32))
    y = y.reshape(bsz, seq, POOL_WIDTH) * pool_scale.astype(jnp.float32)
    return y.astype(z.dtype)


def _hybrid_mixer(u, w_in, rwkv_mu, rwkv_w0, rwkv_w2, rwkv_a0, rwkv_a2, rwkv_g2, rwkv_k_k,
                  rwkv_k_a, rwkv_r_k, rwkv_gn_w, rwkv_gn_b, w_proj_a, pool_w, pool_scale,
                  w_proj_b, w_out):
    p = u @ w_in
    z_a, z_b, g_a, g_b = jnp.split(p, IN_SPLITS, axis=-1)
    y_a = _rwkv7_branch(z_a, rwkv_mu, rwkv_w0, rwkv_w2, rwkv_a0, rwkv_a2, rwkv_g2,
                        rwkv_k_k, rwkv_k_a, rwkv_r_k, rwkv_gn_w, rwkv_gn_b) @ w_proj_a
    y_b = _pool_branch(z_b, pool_w, pool_scale) @ w_proj_b
    m = jax.nn.sigmoid(g_a) * y_a + jax.nn.sigmoid(g_b) * y_b
    return m @ w_out


def setup_inputs(seed: int = 0) -> dict:
    key = jax.random.key(seed)
    ks = iter(jax.random.split(key, 40))
    L, D, W = DEPTH, D_MODEL, RWKV_WIDTH

    def nrm(shape, scale):
        return jax.random.normal(next(ks), shape, jnp.float32) * scale

    def gain(shape):
        return 1.0 + nrm(shape, 0.1)

    return {
        'x': nrm((BATCH, SEQ, D), 1.0),
        'ln_ffn1_pre': gain((L, D)),
        'ln_ffn1_post': gain((L, D)),
        'ffn1_gate': nrm((L, D, D_FF), D ** -0.5),
        'ffn1_up': nrm((L, D, D_FF), D ** -0.5),
        'ffn1_down': nrm((L, D_FF, D), D_FF ** -0.5),
        'ln_mix_pre': gain((L, D)),
        'ln_mix_post': gain((L, D)),
        'w_in': nrm((L, D, IN_COLS), D ** -0.5),
        'rwkv_mu': jax.random.uniform(next(ks), (L, RWKV_COLS), jnp.float32),
        'rwkv_w0': jax.random.uniform(next(ks), (L, W), jnp.float32, -6.0, -0.5),
        'rwkv_w2': nrm((L, DECAY_LORA, W), 0.1),
        'rwkv_a0': nrm((L, W), 0.3),
        'rwkv_a2': nrm((L, AAA_LORA, W), 0.1),
        'rwkv_g2': nrm((L, GATE_LORA, W), GATE_LORA ** -0.5),
        'rwkv_k_k': 0.85 + nrm((L, W), 0.1),
        'rwkv_k_a': gain((L, W)),
        'rwkv_r_k': nrm((L, RWKV_HEADS, RWKV_HEAD_DIM), 0.1),
        'rwkv_gn_w': gain((L, W)),
        'rwkv_gn_b': nrm((L, W), 0.01),
        'w_proj_a': nrm((L, W, D), W ** -0.5),
        'pool_w': nrm((L, POOL_GROUPS, POOL_GROUP_DIM, POOL_GROUP_DIM), POOL_GROUP_DIM ** -0.5),
        'pool_scale': gain((L, POOL_WIDTH)),
        'w_proj_b': nrm((L, POOL_WIDTH, D), POOL_WIDTH ** -0.5),
        'w_out': nrm((L, D, D), D ** -0.5),
        'ln_ffn2_pre': gain((L, D)),
        'ln_ffn2_post': gain((L, D)),
        'ffn2_gate': nrm((L, D, D_FF), D ** -0.5),
        'ffn2_up': nrm((L, D, D_FF), D ** -0.5),
        'ffn2_down': nrm((L, D_FF, D), D_FF ** -0.5),
    }


def reference(x, ln_ffn1_pre, ln_ffn1_post, ffn1_gate, ffn1_up, ffn1_down, ln_mix_pre,
              ln_mix_post, w_in, rwkv_mu, rwkv_w0, rwkv_w2, rwkv_a0, rwkv_a2, rwkv_g2,
              rwkv_k_k, rwkv_k_a, rwkv_r_k, rwkv_gn_w, rwkv_gn_b, w_proj_a, pool_w, pool_scale,
              w_proj_b, w_out, ln_ffn2_pre, ln_ffn2_post, ffn2_gate, ffn2_up, ffn2_down):
    h = x
    for l in range(DEPTH):
        f = _swiglu(_rmsnorm(h, ln_ffn1_pre[l]), ffn1_gate[l], ffn1_up[l], ffn1_down[l])
        h = h + MACARON_WEIGHT * _rmsnorm(f, ln_ffn1_post[l])
        mx = _hybrid_mixer(_rmsnorm(h, ln_mix_pre[l]), w_in[l], rwkv_mu[l], rwkv_w0[l],
                           rwkv_w2[l], rwkv_a0[l], rwkv_a2[l], rwkv_g2[l], rwkv_k_k[l],
                           rwkv_k_a[l], rwkv_r_k[l], rwkv_gn_w[l], rwkv_gn_b[l], w_proj_a[l],
                           pool_w[l], pool_scale[l], w_proj_b[l], w_out[l])
        h = h + _rmsnorm(mx, ln_mix_post[l])
        f = _swiglu(_rmsnorm(h, ln_ffn2_pre[l]), ffn2_gate[l], ffn2_up[l], ffn2_down[l])
        h = h + MACARON_WEIGHT * _rmsnorm(f, ln_ffn2_post[l])
    return h
```

```python
import functools

import jax
import jax.numpy as jnp
from jax import lax
from jax.experimental import pallas as pl
from jax.experimental.pallas import tpu as pltpu

F32 = jnp.float32
BF16 = jnp.bfloat16

LANES = 128
VMEM_LIMIT_BYTES = 56 * 1024 * 1024

NORM_EPS = 1e-6
GN_EPS = 64e-5
L2_EPS = 1e-12
MACARON_WEIGHT = 0.5

HEAD_DIM = 64
POOL_WINDOWS = (2, 4, 8, 16)
POOL_HALO = 16
CHUNK = 64
PAIR = 2 * HEAD_DIM
LAT_DECAY, LAT_AAA, LAT_GATE = 128, 128, 256
LAT_COLS = LAT_DECAY + LAT_AAA + LAT_GATE


def _dot(a, b):
    return jnp.dot(a, b, preferred_element_type=F32)


def _dot_nt(a, b):
    return lax.dot_general(a, b, (((1,), (1,)), ((), ())), preferred_element_type=F32)


def _dot_tn(a, b):
    return lax.dot_general(a, b, (((0,), (0,)), ((), ())), preferred_element_type=F32)


def _split2(x):
    hi = x.astype(BF16)
    lo = (x - hi.astype(F32)).astype(BF16)
    return hi, lo


def _dot_f32lhs(x, w_bf16):
    hi, lo = _split2(x)
    return _dot(hi, w_bf16) + _dot(lo, w_bf16)


def _rms(x, gain):
    ms = jnp.mean(x * x, axis=-1, keepdims=True)
    return x * lax.rsqrt(ms + NORM_EPS) * gain


def _sigmoid(x):
    return 1.0 / (1.0 + jnp.exp(-x))


def _params(semantics):
    return pltpu.CompilerParams(dimension_semantics=semantics, vmem_limit_bytes=VMEM_LIMIT_BYTES)


def _ffn_kernel(x_ref, gpre_ref, gpost_ref, wg_ref, wu_ref, wd_ref, o_ref, xn_ref, acc_ref):
    j = pl.program_id(1)

    @pl.when(j == 0)
    def _():
        xn_ref[...] = _rms(x_ref[...], gpre_ref[...]).astype(BF16)

    xn = xn_ref[...]
    gate = _dot(xn, wg_ref[...])
    up = _dot(xn, wu_ref[...])
    act = (gate * _sigmoid(gate) * up).astype(BF16)
    contrib = _dot(act, wd_ref[...])

    @pl.when(j == 0)
    def _():
        acc_ref[...] = contrib

    @pl.when(j > 0)
    def _():
        acc_ref[...] += contrib

    @pl.when(j == pl.num_programs(1) - 1)
    def _():
        o_ref[...] = x_ref[...] + MACARON_WEIGHT * _rms(acc_ref[...], gpost_ref[...])


def _ffn(x, g_pre, g_post, w_gate, w_up, w_down, *, tm=512, tf=512):
    t, d = x.shape
    f = w_gate.shape[1]
    return pl.pallas_call(
        _ffn_kernel,
        grid=(t // tm, f // tf),
        in_specs=[
            pl.BlockSpec((tm, d), lambda i, j: (i, 0)),
            pl.BlockSpec((1, d), lambda i, j: (0, 0)),
            pl.BlockSpec((1, d), lambda i, j: (0, 0)),
            pl.BlockSpec((d, tf), lambda i, j: (0, j)),
            pl.BlockSpec((d, tf), lambda i, j: (0, j)),
            pl.BlockSpec((tf, d), lambda i, j: (j, 0)),
        ],
        out_specs=pl.BlockSpec((tm, d), lambda i, j: (i, 0)),
        out_shape=jax.ShapeDtypeStruct((t, d), F32),
        scratch_shapes=[pltpu.VMEM((tm, d), BF16), pltpu.VMEM((tm, d), F32)],
        compiler_params=_params(("parallel", "arbitrary")),
        name="ffn",
    )(x, g_pre, g_post, w_gate, w_up, w_down)


def _inproj_kernel(h_ref, g_ref, w_ref, o_ref, u_ref):
    @pl.when(pl.program_id(1) == 0)
    def _():
        u_ref[...] = _rms(h_ref[...], g_ref[...]).astype(BF16)

    o_ref[...] = _dot(u_ref[...], w_ref[...])


def _inproj(h, gain, w, *, tm=512, tn=512):
    t, d = h.shape
    n = w.shape[1]
    return pl.pallas_call(
        _inproj_kernel,
        grid=(t // tm, n // tn),
        in_specs=[
            pl.BlockSpec((tm, d), lambda i, j: (i, 0)),
            pl.BlockSpec((1, d), lambda i, j: (0, 0)),
            pl.BlockSpec((d, tn), lambda i, j: (0, j)),
        ],
        out_specs=pl.BlockSpec((tm, tn), lambda i, j: (i, j)),
        out_shape=jax.ShapeDtypeStruct((t, n), F32),
        scratch_shapes=[pltpu.VMEM((tm, d), BF16)],
        compiler_params=_params(("parallel", "arbitrary")),
        name="inproj",
    )(h, gain, w)


def _token_shift_lerp(z, prev_row, mu):
    rolled = pltpu.roll(z, 1, 0)
    row = lax.broadcasted_iota(jnp.int32, z.shape, 0)
    shifted = jnp.where(row == 0, prev_row, rolled)
    return z + (shifted - z) * mu


def _rwkv_kernel(zr_ref, zl_ref, mur_ref, mul_ref, w0_ref, w2_ref, a0_ref, a2_ref, g2_ref,
                 kk_ref, ka_ref, rk_ref, gnw_ref, gnb_ref, hsum_ref, hexp_ref,
                 o_ref,
                 prev_r_ref, prev_l_ref, state_ref,
                 r_s, k_s, v_s, a_s, b_s, ld_s, y_s, *, width, ct):
    i = pl.program_id(1)
    n_pairs = width // PAIR
    c2 = 2 * CHUNK

    @pl.when(i == 0)
    def _():
        prev_r_ref[...] = jnp.zeros_like(prev_r_ref)
        prev_l_ref[...] = jnp.zeros_like(prev_l_ref)
        state_ref[...] = jnp.zeros_like(state_ref)

    zr = zr_ref[...]
    zl = zl_ref[...]
    zsr = _token_shift_lerp(zr, prev_r_ref[...], mur_ref[...])
    zsl = _token_shift_lerp(zl, prev_l_ref[...], mul_ref[...])
    prev_r_ref[...] = zr[ct - 1:ct, :]
    prev_l_ref[...] = zl[ct - 1:ct, :]

    r = zsr[:, 0:width]
    k = zsr[:, width:2 * width]
    v = zsr[:, 2 * width:3 * width]
    lw = zsl[:, 0:LAT_DECAY]
    la = zsl[:, LAT_DECAY:LAT_DECAY + LAT_AAA]
    lg = zsl[:, LAT_DECAY + LAT_AAA:LAT_COLS]

    hsum = hsum_ref[...]
    hexp = hexp_ref[...]

    def head_sum(x):
        return _dot_f32lhs(_dot_f32lhs(x, hsum), hexp)

    wpre = w0_ref[...] + _dot(jnp.tanh(lw).astype(BF16), w2_ref[...])
    neg = -wpre
    softplus = jnp.maximum(neg, 0.0) + jnp.log(1.0 + jnp.exp(-jnp.abs(neg)))
    log_decay = -jnp.exp(-softplus - 0.5)
    a = _sigmoid(a0_ref[...] + _dot(la.astype(BF16), a2_ref[...]))
    g = _dot(_sigmoid(lg).astype(BF16), g2_ref[...])
    kk = k * kk_ref[...]
    kk = kk / jnp.maximum(jnp.sqrt(head_sum(kk * kk)), L2_EPS)
    k2 = k * (1.0 + (a - 1.0) * ka_ref[...])
    bonus = head_sum(r * k2 * rk_ref[...]) * v

    r_s[...] = r
    k_s[...] = k2
    v_s[...] = v
    a_s[...] = -kk
    b_s[...] = kk * a
    ld_s[...] = log_decay

    ri = lax.broadcasted_iota(jnp.int32, (c2, c2), 0)
    ci = lax.broadcasted_iota(jnp.int32, (c2, c2), 1)
    same_head = (ri // CHUNK) == (ci // CHUNK)
    strict = same_head & (ri > ci)
    incl = same_head & (ri >= ci)
    eye = (ri == ci).astype(F32)
    level_masks = []
    s = 1
    while s < CHUNK:
        level_masks.append(((ri // (2 * s)) == (ci // (2 * s))) & (((ri // s) % 2) == 1) & (((ci // s) % 2) == 0))
        s *= 2
    tri_r = lax.broadcasted_iota(jnp.int32, (CHUNK, CHUNK), 0)
    tri_c = lax.broadcasted_iota(jnp.int32, (CHUNK, CHUNK), 1)
    cum_tri = (tri_r >= tri_c).astype(BF16)
    lane = lax.broadcasted_iota(jnp.int32, (CHUNK, PAIR), 1)
    lo = lane < HEAD_DIM
    diag_pp = lax.broadcasted_iota(jnp.int32, (PAIR, PAIR), 0) == lax.broadcasted_iota(jnp.int32, (PAIR, PAIR), 1)

    def stack(x):
        return jnp.concatenate([jnp.where(lo, x, 0.0), jnp.where(lo, 0.0, x)], axis=0)

    def chunk_body(c, carry):
        rows = pl.ds(pl.multiple_of(c * CHUNK, CHUNK), CHUNK)
        ld = ld_s[rows, :]
        l1 = ld.astype(BF16)
        l2f = ld - l1.astype(F32)
        l2 = l2f.astype(BF16)
        l3 = (l2f - l2.astype(F32)).astype(BF16)
        gcum = _dot(cum_tri, l1) + _dot(cum_tri, l2) + _dot(cum_tri, l3)
        gtot = gcum[CHUNK - 1:CHUNK, :]
        e_pos = jnp.exp(gcum)
        e_prev = jnp.exp(gcum - ld)
        e_neg = jnp.exp(-gcum)
        e_end = jnp.exp(gtot - gcum)
        rr = r_s[rows, :]
        kc = k_s[rows, :]
        vc = v_s[rows, :]
        ac = a_s[rows, :]
        bc = b_s[rows, :]
        at_all = ac * e_prev
        rt_all = rr * e_pos
        bt_all = bc * e_neg
        kt_all = kc * e_neg
        bh_all = bc * e_end
        kh_all = kc * e_end
        g_end = jnp.exp(gtot)

        for p in range(n_pairs):
            ls = slice(p * PAIR, (p + 1) * PAIR)
            rt_f = stack(rt_all[:, ls])
            at = stack(at_all[:, ls]).astype(BF16)
            rt = rt_f.astype(BF16)
            bt = stack(bt_all[:, ls]).astype(BF16)
            kt = stack(kt_all[:, ls]).astype(BF16)
            vs = stack(vc[:, ls]).astype(BF16)
            bh = stack(bh_all[:, ls]).astype(BF16)
            kh = stack(kh_all[:, ls]).astype(BF16)

            scores = _dot_nt(jnp.concatenate([at, rt], axis=0), jnp.concatenate([bt, kt], axis=0))
            a_ab = jnp.where(strict, scores[0:c2, 0:c2], 0.0)
            a_ak = jnp.where(strict, scores[0:c2, c2:2 * c2], 0.0).astype(BF16)
            a_rb = jnp.where(incl, scores[c2:2 * c2, 0:c2], 0.0).astype(BF16)
            a_rk = jnp.where(incl, scores[c2:2 * c2, c2:2 * c2], 0.0).astype(BF16)

            tinv = eye + jnp.where(level_masks[0], a_ab, 0.0)
            for m in level_masks[1:]:
                tb = tinv.astype(BF16)
                tinv = tinv + _dot(tb, _dot(jnp.where(m, a_ab, 0.0).astype(BF16), tb).astype(BF16))
            tb = tinv.astype(BF16)

            akv = _dot(a_ak, vs).astype(BF16)
            wu = _dot(tb, jnp.concatenate([at, akv], axis=1))
            wub = wu.astype(BF16)
            qy = _dot(a_rb, wub)
            qm = rt_f + qy[:, 0:PAIR]
            yv = qy[:, PAIR:2 * PAIR] + _dot(a_rk, vs)
            pz = _dot_tn(bh, wub)
            p_mat = jnp.where(diag_pp, g_end[:, ls], 0.0) + pz[:, 0:PAIR]
            z_mat = pz[:, PAIR:2 * PAIR] + _dot_tn(kh, vs)

            qm_pair = (qm[0:CHUNK, :] + qm[CHUNK:c2, :]).astype(BF16)
            yv_pair = yv[0:CHUNK, :] + yv[CHUNK:c2, :]
            h_prev = state_ref[p].astype(BF16)
            y_s[rows, ls] = _dot(qm_pair, h_prev) + yv_pair
            state_ref[p] = _dot(p_mat.astype(BF16), h_prev) + z_mat
        return carry

    lax.fori_loop(0, ct // CHUNK, chunk_body, 0)

    y = y_s[...]
    mean = head_sum(y) * (1.0 / HEAD_DIM)
    d = y - mean
    var = head_sum(d * d) * (1.0 / HEAD_DIM)
    yn = d * lax.rsqrt(var + GN_EPS) * gnw_ref[...] + gnb_ref[...]
    o_ref[...] = ((yn + bonus) * g).astype(o_ref.dtype)


def _rwkv(p, batch, seq, lat_block, mu_r, mu_l, w0, w2p, a0, a2p, g2p, k_k, k_a, r_k, gn_w, gn_b, *, ct=256):
    width = w0.shape[1]
    n_heads = width // HEAD_DIM
    steps = seq // ct
    head_of_lane = jnp.arange(width, dtype=jnp.int32) // HEAD_DIM
    hsum = (head_of_lane[:, None] == jnp.arange(LANES, dtype=jnp.int32)[None, :]).astype(BF16)
    hexp = hsum.T
    row = lambda shape: pl.BlockSpec(shape, lambda b, i: (0, 0))
    kernel = functools.partial(_rwkv_kernel, width=width, ct=ct)
    tile = (ct, width)
    return pl.pallas_call(
        kernel,
        grid=(batch, steps),
        in_specs=[
            pl.BlockSpec((ct, 3 * width), lambda b, i: (b * steps + i, 0)),
            pl.BlockSpec((ct, LAT_COLS), lambda b, i: (b * steps + i, lat_block)),
            row((1, 3 * width)), row((1, LAT_COLS)),
            row((1, width)), row((LAT_DECAY, width)),
            row((1, width)), row((LAT_AAA, width)),
            row((LAT_GATE, width)),
            row((1, width)), row((1, width)), row((1, width)), row((1, width)), row((1, width)),
            row((width, LANES)), row((LANES, width)),
        ],
        out_specs=pl.BlockSpec((ct, width), lambda b, i: (b * steps + i, 0)),
        out_shape=jax.ShapeDtypeStruct((batch * seq, width), BF16),
        scratch_shapes=[
            pltpu.VMEM((1, 3 * width), F32), pltpu.VMEM((1, LAT_COLS), F32),
            pltpu.VMEM((n_heads // 2, PAIR, PAIR), F32),
            pltpu.VMEM(tile, F32), pltpu.VMEM(tile, F32), pltpu.VMEM(tile, F32),
            pltpu.VMEM(tile, F32), pltpu.VMEM(tile, F32), pltpu.VMEM(tile, F32), pltpu.VMEM(tile, F32),
        ],
        compiler_params=_params(("arbitrary", "arbitrary")),
        name="rwkv",
    )(p, p, mu_r, mu_l, w0, w2p, a0, a2p, g2p, k_k, k_a, r_k, gn_w, gn_b, hsum, hexp)


def _pool_kernel(z_ref, w_ref, scale_ref, o_ref, halo_ref, *, tp, group_dim):
    i = pl.program_id(1)

    @pl.when(i == 0)
    def _():
        halo_ref[...] = jnp.zeros_like(halo_ref)

    z = z_ref[...]
    ext = jnp.concatenate([halo_ref[...], z], axis=0)
    halo_ref[...] = z[tp - POOL_HALO:tp, :]
    pos = (i * tp + lax.broadcasted_iota(jnp.int32, (tp, group_dim), 0) + 1).astype(F32)

    run = ext
    span = 1
    for gi, win in enumerate(POOL_WINDOWS):
        cols = slice(gi * group_dim, (gi + 1) * group_dim)
        while span < win:
            run = run + pltpu.roll(run, span, 0)
            span *= 2
        total = run[POOL_HALO:, cols]
        pooled = total / jnp.minimum(pos, float(win))
        mixed = (pooled - z[:, cols]).astype(BF16)
        o_ref[:, cols] = (_dot(mixed, w_ref[gi]) * scale_ref[:, cols]).astype(o_ref.dtype)


def _pool(p, batch, seq, col_block, pool_w, pool_scale, *, tp=512):
    groups, group_dim, _ = pool_w.shape
    width = groups * group_dim
    steps = seq // tp
    kernel = functools.partial(_pool_kernel, tp=tp, group_dim=group_dim)
    return pl.pallas_call(
        kernel,
        grid=(batch, steps),
        in_specs=[
            pl.BlockSpec((tp, width), lambda b, i: (b * steps + i, col_block)),
            pl.BlockSpec((groups, group_dim, group_dim), lambda b, i: (0, 0, 0)),
            pl.BlockSpec((1, width), lambda b, i: (0, 0)),
        ],
        out_specs=pl.BlockSpec((tp, width), lambda b, i: (b * steps + i, 0)),
        out_shape=jax.ShapeDtypeStruct((batch * seq, width), BF16),
        scratch_shapes=[pltpu.VMEM((POOL_HALO, width), F32)],
        compiler_params=_params(("arbitrary", "arbitrary")),
        name="pool",
    )(p, pool_w, pool_scale)


def _mix_kernel(ya_ref, yb_ref, ga_ref, gb_ref, h_ref, wa_ref, wb_ref, wo_ref, gpost_ref, o_ref):
    pa = _dot(ya_ref[...], wa_ref[...])
    pb = _dot(yb_ref[...], wb_ref[...])
    m = (_sigmoid(ga_ref[...]) * pa + _sigmoid(gb_ref[...]) * pb).astype(BF16)
    mx = _dot(m, wo_ref[...])
    o_ref[...] = h_ref[...] + _rms(mx, gpost_ref[...])


def _mix_out(ya, yb, p, ga_block, gb_block, h, wa, wb, wo, g_post, *, tm=256):
    t, d = h.shape
    wa_rows, wb_rows = wa.shape[0], wb.shape[0]
    const = lambda shape: pl.BlockSpec(shape, lambda i: (0, 0), pipeline_mode=pl.Buffered(1))
    return pl.pallas_call(
        _mix_kernel,
        grid=(t // tm,),
        in_specs=[
            pl.BlockSpec((tm, wa_rows), lambda i: (i, 0)),
            pl.BlockSpec((tm, wb_rows), lambda i: (i, 0)),
            pl.BlockSpec((tm, d), lambda i: (i, ga_block)),
            pl.BlockSpec((tm, d), lambda i: (i, gb_block)),
            pl.BlockSpec((tm, d), lambda i: (i, 0)),
            const((wa_rows, d)), const((wb_rows, d)), const((d, d)),
            pl.BlockSpec((1, d), lambda i: (0, 0)),
        ],
        out_specs=pl.BlockSpec((tm, d), lambda i: (i, 0)),
        out_shape=jax.ShapeDtypeStruct((t, d), F32),
        compiler_params=_params(("parallel",)),
        name="mix_out",
    )(ya, yb, p, p, h, wa, wb, wo, g_post)


def _pad_cols(w, n):
    return jnp.pad(w, ((0, 0), (0, n - w.shape[1])))


def _pad_rows(w, n):
    return jnp.pad(w, ((0, n - w.shape[0]), (0, 0)))


def kernel(x, ln_ffn1_pre, ln_ffn1_post, ffn1_gate, ffn1_up, ffn1_down, ln_mix_pre, ln_mix_post, w_in, rwkv_mu, rwkv_w0, rwkv_w2, rwkv_a0, rwkv_a2, rwkv_g2, rwkv_k_k, rwkv_k_a, rwkv_r_k, rwkv_gn_w, rwkv_gn_b, w_proj_a, pool_w, pool_scale, w_proj_b, w_out, ln_ffn2_pre, ln_ffn2_post, ffn2_gate, ffn2_up, ffn2_down):
    batch, seq, d = x.shape
    depth = w_in.shape[0]
    width = rwkv_w0.shape[1]
    pool_width = pool_scale.shape[1]
    n_decay, n_aaa, n_gate = rwkv_w2.shape[1], rwkv_a2.shape[1], rwkv_g2.shape[1]
    rkv = 3 * width
    rwkv_cols = rkv + n_decay + n_aaa + n_gate
    assert n_decay <= LAT_DECAY and n_aaa <= LAT_AAA and n_gate <= LAT_GATE
    assert rkv % pool_width == 0 and (rkv + pool_width) % d == 0 and (rkv + pool_width + 2 * d) % LAT_COLS == 0

    h = x.reshape(batch * seq, d)
    row = lambda v: v.reshape(1, -1)
    for l in range(depth):
        h = _ffn(h, row(ln_ffn1_pre[l]), row(ln_ffn1_post[l]),
                 ffn1_gate[l].astype(BF16), ffn1_up[l].astype(BF16), ffn1_down[l].astype(BF16))

        wl = w_in[l]
        o1, o2 = rkv + n_decay, rkv + n_decay + n_aaa
        w_cat = jnp.concatenate([
            wl[:, :rkv], wl[:, rwkv_cols:],
            _pad_cols(wl[:, rkv:o1], LAT_DECAY), _pad_cols(wl[:, o1:o2], LAT_AAA), _pad_cols(wl[:, o2:rwkv_cols], LAT_GATE),
        ], axis=1).astype(BF16)
        mu = rwkv_mu[l]
        mu_l = jnp.concatenate([
            jnp.pad(mu[rkv:o1], (0, LAT_DECAY - n_decay)), jnp.pad(mu[o1:o2], (0, LAT_AAA - n_aaa)),
            jnp.pad(mu[o2:], (0, LAT_GATE - n_gate))])
        p = _inproj(h, row(ln_mix_pre[l]), w_cat)

        pool_block = rkv // pool_width
        ga_block = (rkv + pool_width) // d
        lat_block = (rkv + pool_width + 2 * d) // LAT_COLS
        ya = _rwkv(p, batch, seq, lat_block, row(mu[:rkv]), row(mu_l),
                   row(rwkv_w0[l]), _pad_rows(rwkv_w2[l], LAT_DECAY).astype(BF16),
                   row(rwkv_a0[l]), _pad_rows(rwkv_a2[l], LAT_AAA).astype(BF16),
                   _pad_rows(rwkv_g2[l], LAT_GATE).astype(BF16),
                   row(rwkv_k_k[l]), row(rwkv_k_a[l]), row(rwkv_r_k[l]), row(rwkv_gn_w[l]), row(rwkv_gn_b[l]))
        yb = _pool(p, batch, seq, pool_block, pool_w[l].astype(BF16), row(pool_scale[l]))
        h = _mix_out(ya, yb, p, ga_block, ga_block + 1, h,
                     w_proj_a[l].astype(BF16), w_proj_b[l].astype(BF16), w_out[l].astype(BF16), row(ln_mix_post[l]))

        h = _ffn(h, row(ln_ffn2_pre[l]), row(ln_ffn2_post[l]),
                 ffn2_gate[l].astype(BF16), ffn2_up[l].astype(BF16), ffn2_down[l].astype(BF16))
    return h.reshape(batch, seq, d)
```

```python
import functools

import jax
import jax.numpy as jnp
from jax import lax
from jax.experimental import pallas as pl
from jax.experimental.pallas import tpu as pltpu

F32 = jnp.float32
BF16 = jnp.bfloat16

LANES = 128
VMEM_LIMIT_BYTES = 56 * 1024 * 1024

NORM_EPS = 1e-6
GN_EPS = 64e-5
L2_EPS = 1e-12
MACARON_WEIGHT = 0.5

HEAD_DIM = 64
POOL_WINDOWS = (2, 4, 8, 16)
POOL_HALO = 16
CHUNK = 64
PAIR = 2 * HEAD_DIM
LAT_DECAY, LAT_AAA, LAT_GATE = 128, 128, 256
LAT_COLS = LAT_DECAY + LAT_AAA + LAT_GATE


def _dot(a, b):
    return jnp.dot(a, b, preferred_element_type=F32)


def _dot_nt(a, b):
    return lax.dot_general(a, b, (((1,), (1,)), ((), ())), preferred_element_type=F32)


def _dot_tn(a, b):
    return lax.dot_general(a, b, (((0,), (0,)), ((), ())), preferred_element_type=F32)


def _split2(x):
    hi = x.astype(BF16)
    lo = (x - hi.astype(F32)).astype(BF16)
    return hi, lo


def _dot_f32lhs(x, w_bf16):
    hi, lo = _split2(x)
    return _dot(hi, w_bf16) + _dot(lo, w_bf16)


def _rms(x, gain):
    ms = jnp.mean(x * x, axis=-1, keepdims=True)
    return x * lax.rsqrt(ms + NORM_EPS) * gain


def _sigmoid(x):
    return 1.0 / (1.0 + jnp.exp(-x))


def _params(semantics):
    return pltpu.CompilerParams(dimension_semantics=semantics, vmem_limit_bytes=VMEM_LIMIT_BYTES)


def _ffn_kernel(x_ref, gpre_ref, gpost_ref, wg_ref, wu_ref, wd_ref, o_ref, xn_ref, acc_ref):
    j = pl.program_id(1)

    @pl.when(j == 0)
    def _():
        xn_ref[...] = _rms(x_ref[...], gpre_ref[...]).astype(BF16)
        acc_ref[...] = jnp.zeros_like(acc_ref)

    xn = xn_ref[...]
    gate = _dot(xn, wg_ref[...])
    up = _dot(xn, wu_ref[...])
    act = (gate * _sigmoid(gate) * up).astype(BF16)
    acc_ref[...] += _dot(act, wd_ref[...])

    @pl.when(j == pl.num_programs(1) - 1)
    def _():
        o_ref[...] = x_ref[...] + MACARON_WEIGHT * _rms(acc_ref[...], gpost_ref[...])


def _ffn(x, g_pre, g_post, w_gate, w_up, w_down, *, tm=512, tf=512):
    t, d = x.shape
    f = w_gate.shape[1]
    return pl.pallas_call(
        _ffn_kernel,
        grid=(t // tm, f // tf),
        in_specs=[
            pl.BlockSpec((tm, d), lambda i, j: (i, 0)),
            pl.BlockSpec((1, d), lambda i, j: (0, 0)),
            pl.BlockSpec((1, d), lambda i, j: (0, 0)),
            pl.BlockSpec((d, tf), lambda i, j: (0, j)),
            pl.BlockSpec((d, tf), lambda i, j: (0, j)),
            pl.BlockSpec((tf, d), lambda i, j: (j, 0)),
        ],
        out_specs=pl.BlockSpec((tm, d), lambda i, j: (i, 0)),
        out_shape=jax.ShapeDtypeStruct((t, d), F32),
        scratch_shapes=[pltpu.VMEM((tm, d), BF16), pltpu.VMEM((tm, d), F32)],
        compiler_params=_params(("parallel", "arbitrary")),
        name="ffn",
    )(x, g_pre, g_post, w_gate, w_up, w_down)


def _inproj_kernel(h_ref, g_ref, w_ref, o_ref, u_ref):
    @pl.when(pl.program_id(1) == 0)
    def _():
        u_ref[...] = _rms(h_ref[...], g_ref[...]).astype(BF16)

    o_ref[...] = _dot(u_ref[...], w_ref[...])


def _inproj(h, gain, w, *, tm=1024, tn=512):
    t, d = h.shape
    n = w.shape[1]
    return pl.pallas_call(
        _inproj_kernel,
        grid=(t // tm, n // tn),
        in_specs=[
            pl.BlockSpec((tm, d), lambda i, j: (i, 0)),
            pl.BlockSpec((1, d), lambda i, j: (0, 0)),
            pl.BlockSpec((d, tn), lambda i, j: (0, j)),
        ],
        out_specs=pl.BlockSpec((tm, tn), lambda i, j: (i, j)),
        out_shape=jax.ShapeDtypeStruct((t, n), F32),
        scratch_shapes=[pltpu.VMEM((tm, d), BF16)],
        compiler_params=_params(("parallel", "arbitrary")),
        name="inproj",
    )(h, gain, w)


def _token_shift_lerp(z, prev_row, mu):
    rolled = pltpu.roll(z, 1, 0)
    row = lax.broadcasted_iota(jnp.int32, z.shape, 0)
    shifted = jnp.where(row == 0, prev_row, rolled)
    return z + (shifted - z) * mu


def _rwkv_kernel(zr_ref, zl_ref, mur_ref, mul_ref, w0_ref, w2_ref, a0_ref, a2_ref, g2_ref,
                 kk_ref, ka_ref, rk_ref, gnw_ref, gnb_ref, hsum_ref, hexp_ref,
                 o_ref,
                 prev_r_ref, prev_l_ref, state_ref,
                 r_s, k_s, v_s, a_s, b_s, ld_s, y_s, *, width, ct):
    i = pl.program_id(1)
    n_pairs = width // PAIR
    c2 = 2 * CHUNK

    @pl.when(i == 0)
    def _():
        prev_r_ref[...] = jnp.zeros_like(prev_r_ref)
        prev_l_ref[...] = jnp.zeros_like(prev_l_ref)
        state_ref[...] = jnp.zeros_like(state_ref)

    zr = zr_ref[...]
    zl = zl_ref[...]
    zsr = _token_shift_lerp(zr, prev_r_ref[...], mur_ref[...])
    zsl = _token_shift_lerp(zl, prev_l_ref[...], mul_ref[...])
    prev_r_ref[...] = zr[ct - 1:ct, :]
    prev_l_ref[...] = zl[ct - 1:ct, :]

    r = zsr[:, 0:width]
    k = zsr[:, width:2 * width]
    v = zsr[:, 2 * width:3 * width]
    lw = zsl[:, 0:LAT_DECAY]
    la = zsl[:, LAT_DECAY:LAT_DECAY + LAT_AAA]
    lg = zsl[:, LAT_DECAY + LAT_AAA:LAT_COLS]

    hsum = hsum_ref[...]
    hexp = hexp_ref[...]

    def head_sum(x):
        return _dot_f32lhs(_dot_f32lhs(x, hsum), hexp)

    wpre = w0_ref[...] + _dot(jnp.tanh(lw).astype(BF16), w2_ref[...])
    neg = -wpre
    softplus = jnp.maximum(neg, 0.0) + jnp.log(1.0 + jnp.exp(-jnp.abs(neg)))
    log_decay = -jnp.exp(-softplus - 0.5)
    a = _sigmoid(a0_ref[...] + _dot(la.astype(BF16), a2_ref[...]))
    g = _dot(_sigmoid(lg).astype(BF16), g2_ref[...])
    kk = k * kk_ref[...]
    kk = kk / jnp.maximum(jnp.sqrt(head_sum(kk * kk)), L2_EPS)
    k2 = k * (1.0 + (a - 1.0) * ka_ref[...])
    bonus = head_sum(r * k2 * rk_ref[...]) * v

    r_s[...] = r
    k_s[...] = k2
    v_s[...] = v
    a_s[...] = -kk
    b_s[...] = kk * a
    ld_s[...] = log_decay

    ri = lax.broadcasted_iota(jnp.int32, (c2, c2), 0)
    ci = lax.broadcasted_iota(jnp.int32, (c2, c2), 1)
    same_head = (ri // CHUNK) == (ci // CHUNK)
    strict = same_head & (ri > ci)
    incl = same_head & (ri >= ci)
    eye = (ri == ci).astype(F32)
    level_masks = []
    s = 1
    while s < CHUNK:
        level_masks.append(((ri // (2 * s)) == (ci // (2 * s))) & (((ri // s) % 2) == 1) & (((ci // s) % 2) == 0))
        s *= 2
    tri_r = lax.broadcasted_iota(jnp.int32, (CHUNK, CHUNK), 0)
    tri_c = lax.broadcasted_iota(jnp.int32, (CHUNK, CHUNK), 1)
    cum_tri = (tri_r >= tri_c).astype(BF16)
    lane = lax.broadcasted_iota(jnp.int32, (CHUNK, PAIR), 1)
    lo = lane < HEAD_DIM
    diag_pp = lax.broadcasted_iota(jnp.int32, (PAIR, PAIR), 0) == lax.broadcasted_iota(jnp.int32, (PAIR, PAIR), 1)

    def stack(x):
        return jnp.concatenate([jnp.where(lo, x, 0.0), jnp.where(lo, 0.0, x)], axis=0)

    def chunk_body(c, carry):
        rows = pl.ds(pl.multiple_of(c * CHUNK, CHUNK), CHUNK)
        ld = ld_s[rows, :]
        l1 = ld.astype(BF16)
        l2f = ld - l1.astype(F32)
        l2 = l2f.astype(BF16)
        l3 = (l2f - l2.astype(F32)).astype(BF16)
        gcum = _dot(cum_tri, l1) + _dot(cum_tri, l2) + _dot(cum_tri, l3)
        gtot = gcum[CHUNK - 1:CHUNK, :]
        e_pos = jnp.exp(gcum)
        e_prev = jnp.exp(gcum - ld)
        e_neg = jnp.exp(-gcum)
        e_end = jnp.exp(gtot - gcum)
        rr = r_s[rows, :]
        kc = k_s[rows, :]
        vc = v_s[rows, :]
        ac = a_s[rows, :]
        bc = b_s[rows, :]
        at_all = ac * e_prev
        rt_all = rr * e_pos
        bt_all = bc * e_neg
        kt_all = kc * e_neg
        bh_all = bc * e_end
        kh_all = kc * e_end
        g_end = jnp.exp(gtot)

        pairs = range(n_pairs)
        ls = [slice(p * PAIR, (p + 1) * PAIR) for p in pairs]
        rt_f = [stack(rt_all[:, s]) for s in ls]
        at = [stack(at_all[:, s]).astype(BF16) for s in ls]
        rt = [x.astype(BF16) for x in rt_f]
        bt = [stack(bt_all[:, s]).astype(BF16) for s in ls]
        kt = [stack(kt_all[:, s]).astype(BF16) for s in ls]
        vs = [stack(vc[:, s]).astype(BF16) for s in ls]
        bh = [stack(bh_all[:, s]).astype(BF16) for s in ls]
        kh = [stack(kh_all[:, s]).astype(BF16) for s in ls]

        scores = [_dot_nt(jnp.concatenate([at[p], rt[p]], axis=0), jnp.concatenate([bt[p], kt[p]], axis=0)) for p in pairs]
        a_ab = [jnp.where(strict, sc[0:c2, 0:c2], 0.0) for sc in scores]
        a_ak = [jnp.where(strict, sc[0:c2, c2:2 * c2], 0.0).astype(BF16) for sc in scores]
        a_rb = [jnp.where(incl, sc[c2:2 * c2, 0:c2], 0.0).astype(BF16) for sc in scores]
        a_rk = [jnp.where(incl, sc[c2:2 * c2, c2:2 * c2], 0.0).astype(BF16) for sc in scores]

        tinv = [eye + jnp.where(level_masks[0], x, 0.0) for x in a_ab]
        for m in level_masks[1:]:
            tb = [t.astype(BF16) for t in tinv]
            inner = [_dot(jnp.where(m, a_ab[p], 0.0).astype(BF16), tb[p]).astype(BF16) for p in pairs]
            tinv = [tinv[p] + _dot(tb[p], inner[p]) for p in pairs]
        tb = [t.astype(BF16) for t in tinv]

        akv = [_dot(a_ak[p], vs[p]).astype(BF16) for p in pairs]
        wub = [_dot(tb[p], jnp.concatenate([at[p], akv[p]], axis=1)).astype(BF16) for p in pairs]
        qy = [_dot(a_rb[p], wub[p]) for p in pairs]
        ykv = [_dot(a_rk[p], vs[p]) for p in pairs]
        pz = [_dot_tn(bh[p], wub[p]) for p in pairs]
        khv = [_dot_tn(kh[p], vs[p]) for p in pairs]
        for p in pairs:
            qm = rt_f[p] + qy[p][:, 0:PAIR]
            yv = qy[p][:, PAIR:2 * PAIR] + ykv[p]
            p_mat = jnp.where(diag_pp, g_end[:, ls[p]], 0.0) + pz[p][:, 0:PAIR]
            z_mat = pz[p][:, PAIR:2 * PAIR] + khv[p]
            qm_pair = (qm[0:CHUNK, :] + qm[CHUNK:c2, :]).astype(BF16)
            yv_pair = yv[0:CHUNK, :] + yv[CHUNK:c2, :]
            h_prev = state_ref[p].astype(BF16)
            y_s[rows, ls[p]] = _dot(qm_pair, h_prev) + yv_pair
            state_ref[p] = _dot(p_mat.astype(BF16), h_prev) + z_mat
        return carry

    lax.fori_loop(0, ct // CHUNK, chunk_body, 0)

    y = y_s[...]
    mean = head_sum(y) * (1.0 / HEAD_DIM)
    d = y - mean
    var = head_sum(d * d) * (1.0 / HEAD_DIM)
    yn = d * lax.rsqrt(var + GN_EPS) * gnw_ref[...] + gnb_ref[...]
    o_ref[...] = ((yn + bonus) * g).astype(o_ref.dtype)


def _rwkv(p, batch, seq, lat_block, mu_r, mu_l, w0, w2p, a0, a2p, g2p, k_k, k_a, r_k, gn_w, gn_b, *, ct=256):
    width = w0.shape[1]
    n_heads = width // HEAD_DIM
    steps = seq // ct
    head_of_lane = jnp.arange(width, dtype=jnp.int32) // HEAD_DIM
    hsum = (head_of_lane[:, None] == jnp.arange(LANES, dtype=jnp.int32)[None, :]).astype(BF16)
    hexp = hsum.T
    row = lambda shape: pl.BlockSpec(shape, lambda b, i: (0, 0))
    kernel = functools.partial(_rwkv_kernel, width=width, ct=ct)
    tile = (ct, width)
    return pl.pallas_call(
        kernel,
        grid=(batch, steps),
        in_specs=[
            pl.BlockSpec((ct, 3 * width), lambda b, i: (b * steps + i, 0)),
            pl.BlockSpec((ct, LAT_COLS), lambda b, i: (b * steps + i, lat_block)),
            row((1, 3 * width)), row((1, LAT_COLS)),
            row((1, width)), row((LAT_DECAY, width)),
            row((1, width)), row((LAT_AAA, width)),
            row((LAT_GATE, width)),
            row((1, width)), row((1, width)), row((1, width)), row((1, width)), row((1, width)),
            row((width, LANES)), row((LANES, width)),
        ],
        out_specs=pl.BlockSpec((ct, width), lambda b, i: (b * steps + i, 0)),
        out_shape=jax.ShapeDtypeStruct((batch * seq, width), BF16),
        scratch_shapes=[
            pltpu.VMEM((1, 3 * width), F32), pltpu.VMEM((1, LAT_COLS), F32),
            pltpu.VMEM((n_heads // 2, PAIR, PAIR), F32),
            pltpu.VMEM(tile, F32), pltpu.VMEM(tile, F32), pltpu.VMEM(tile, F32),
            pltpu.VMEM(tile, F32), pltpu.VMEM(tile, F32), pltpu.VMEM(tile, F32), pltpu.VMEM(tile, F32),
        ],
        compiler_params=_params(("arbitrary", "arbitrary")),
        name="rwkv",
    )(p, p, mu_r, mu_l, w0, w2p, a0, a2p, g2p, k_k, k_a, r_k, gn_w, gn_b, hsum, hexp)


def _pool_kernel(z_ref, w_ref, scale_ref, o_ref, halo_ref, *, tp, group_dim):
    i = pl.program_id(1)

    @pl.when(i == 0)
    def _():
        halo_ref[...] = jnp.zeros_like(halo_ref)

    z = z_ref[...]
    ext = jnp.concatenate([halo_ref[...], z], axis=0)
    halo_ref[...] = z[tp - POOL_HALO:tp, :]
    pos = (i * tp + lax.broadcasted_iota(jnp.int32, (tp, group_dim), 0) + 1).astype(F32)

    run = ext
    span = 1
    for gi, win in enumerate(POOL_WINDOWS):
        cols = slice(gi * group_dim, (gi + 1) * group_dim)
        while span < win:
            run = run + pltpu.roll(run, span, 0)
            span *= 2
        total = run[POOL_HALO:, cols]
        pooled = total / jnp.minimum(pos, float(win))
        mixed = (pooled - z[:, cols]).astype(BF16)
        o_ref[:, cols] = (_dot(mixed, w_ref[gi]) * scale_ref[:, cols]).astype(o_ref.dtype)


def _pool(p, batch, seq, col_block, pool_w, pool_scale, *, tp=512):
    groups, group_dim, _ = pool_w.shape
    width = groups * group_dim
    steps = seq // tp
    kernel = functools.partial(_pool_kernel, tp=tp, group_dim=group_dim)
    return pl.pallas_call(
        kernel,
        grid=(batch, steps),
        in_specs=[
            pl.BlockSpec((tp, width), lambda b, i: (b * steps + i, col_block)),
            pl.BlockSpec((groups, group_dim, group_dim), lambda b, i: (0, 0, 0)),
            pl.BlockSpec((1, width), lambda b, i: (0, 0)),
        ],
        out_specs=pl.BlockSpec((tp, width), lambda b, i: (b * steps + i, 0)),
        out_shape=jax.ShapeDtypeStruct((batch * seq, width), BF16),
        scratch_shapes=[pltpu.VMEM((POOL_HALO, width), F32)],
        compiler_params=_params(("arbitrary", "arbitrary")),
        name="pool",
    )(p, pool_w, pool_scale)


def _mix_kernel(ya_ref, yb_ref, ga_ref, gb_ref, h_ref, wa_ref, wb_ref, wo_ref, gpost_ref, o_ref):
    pa = _dot(ya_ref[...], wa_ref[...])
    pb = _dot(yb_ref[...], wb_ref[...])
    m = (_sigmoid(ga_ref[...]) * pa + _sigmoid(gb_ref[...]) * pb).astype(BF16)
    mx = _dot(m, wo_ref[...])
    o_ref[...] = h_ref[...] + _rms(mx, gpost_ref[...])


def _mix_out(ya, yb, p, ga_block, gb_block, h, wa, wb, wo, g_post, *, tm=256):
    t, d = h.shape
    wa_rows, wb_rows = wa.shape[0], wb.shape[0]
    const = lambda shape: pl.BlockSpec(shape, lambda i: (0, 0), pipeline_mode=pl.Buffered(1))
    return pl.pallas_call(
        _mix_kernel,
        grid=(t // tm,),
        in_specs=[
            pl.BlockSpec((tm, wa_rows), lambda i: (i, 0)),
            pl.BlockSpec((tm, wb_rows), lambda i: (i, 0)),
            pl.BlockSpec((tm, d), lambda i: (i, ga_block)),
            pl.BlockSpec((tm, d), lambda i: (i, gb_block)),
            pl.BlockSpec((tm, d), lambda i: (i, 0)),
            const((wa_rows, d)), const((wb_rows, d)), const((d, d)),
            pl.BlockSpec((1, d), lambda i: (0, 0)),
        ],
        out_specs=pl.BlockSpec((tm, d), lambda i: (i, 0)),
        out_shape=jax.ShapeDtypeStruct((t, d), F32),
        compiler_params=_params(("parallel",)),
        name="mix_out",
    )(ya, yb, p, p, h, wa, wb, wo, g_post)


def _pad_cols(w, n):
    return jnp.pad(w, ((0, 0), (0, n - w.shape[1])))


def _pad_rows(w, n):
    return jnp.pad(w, ((0, n - w.shape[0]), (0, 0)))


def kernel(x, ln_ffn1_pre, ln_ffn1_post, ffn1_gate, ffn1_up, ffn1_down, ln_mix_pre, ln_mix_post, w_in, rwkv_mu, rwkv_w0, rwkv_w2, rwkv_a0, rwkv_a2, rwkv_g2, rwkv_k_k, rwkv_k_a, rwkv_r_k, rwkv_gn_w, rwkv_gn_b, w_proj_a, pool_w, pool_scale, w_proj_b, w_out, ln_ffn2_pre, ln_ffn2_post, ffn2_gate, ffn2_up, ffn2_down):
    batch, seq, d = x.shape
    depth = w_in.shape[0]
    width = rwkv_w0.shape[1]
    pool_width = pool_scale.shape[1]
    n_decay, n_aaa, n_gate = rwkv_w2.shape[1], rwkv_a2.shape[1], rwkv_g2.shape[1]
    rkv = 3 * width
    rwkv_cols = rkv + n_decay + n_aaa + n_gate
    assert n_decay <= LAT_DECAY and n_aaa <= LAT_AAA and n_gate <= LAT_GATE
    assert rkv % pool_width == 0 and (rkv + pool_width) % d == 0 and (rkv + pool_width + 2 * d) % LAT_COLS == 0

    h = x.reshape(batch * seq, d)
    row = lambda v: v.reshape(1, -1)
    for l in range(depth):
        h = _ffn(h, row(ln_ffn1_pre[l]), row(ln_ffn1_post[l]),
                 ffn1_gate[l].astype(BF16), ffn1_up[l].astype(BF16), ffn1_down[l].astype(BF16))

        wl = w_in[l]
        o1, o2 = rkv + n_decay, rkv + n_decay + n_aaa
        w_cat = jnp.concatenate([
            wl[:, :rkv], wl[:, rwkv_cols:],
            _pad_cols(wl[:, rkv:o1], LAT_DECAY), _pad_cols(wl[:, o1:o2], LAT_AAA), _pad_cols(wl[:, o2:rwkv_cols], LAT_GATE),
        ], axis=1).astype(BF16)
        mu = rwkv_mu[l]
        mu_l = jnp.concatenate([
            jnp.pad(mu[rkv:o1], (0, LAT_DECAY - n_decay)), jnp.pad(mu[o1:o2], (0, LAT_AAA - n_aaa)),
            jnp.pad(mu[o2:], (0, LAT_GATE - n_gate))])
        p = _inproj(h, row(ln_mix_pre[l]), w_cat)

        pool_block = rkv // pool_width
        ga_block = (rkv + pool_width) // d
        lat_block = (rkv + pool_width + 2 * d) // LAT_COLS
        ya = _rwkv(p, batch, seq, lat_block, row(mu[:rkv]), row(mu_l),
                   row(rwkv_w0[l]), _pad_rows(rwkv_w2[l], LAT_DECAY).astype(BF16),
                   row(rwkv_a0[l]), _pad_rows(rwkv_a2[l], LAT_AAA).astype(BF16),
                   _pad_rows(rwkv_g2[l], LAT_GATE).astype(BF16),
                   row(rwkv_k_k[l]), row(rwkv_k_a[l]), row(rwkv_r_k[l]), row(rwkv_gn_w[l]), row(rwkv_gn_b[l]))
        yb = _pool(p, batch, seq, pool_block, pool_w[l].astype(BF16), row(pool_scale[l]))
        h = _mix_out(ya, yb, p, ga_block, ga_block + 1, h,
                     w_proj_a[l].astype(BF16), w_proj_b[l].astype(BF16), w_out[l].astype(BF16), row(ln_mix_post[l]))

        h = _ffn(h, row(ln_ffn2_pre[l]), row(ln_ffn2_post[l]),
                 ffn2_gate[l].astype(BF16), ffn2_up[l].astype(BF16), ffn2_down[l].astype(BF16))
    return h.reshape(batch, seq, d)
```

```python
import functools
import itertools

import jax
import jax.numpy as jnp
from jax import lax
from jax.experimental import pallas as pl
from jax.experimental.pallas import tpu as pltpu

F32 = jnp.float32
BF16 = jnp.bfloat16

LANES = 128
SUBLANES = 8
MXU_DIM = 256
VMEM_LIMIT_BYTES = 56 * 1024 * 1024

NORM_EPS = 1e-6
GN_EPS = 64e-5
L2_EPS = 1e-12
DECAY_SCALE = -0.6065306597126334
MACARON_WEIGHT = 0.5

HEAD_DIM = 64
POOL_WINDOWS = (2, 4, 8, 16)
POOL_HALO = 16
CHUNK = 64
LOG_CHUNK = 6
SUBCHUNKS = 2
PAIR = 2 * HEAD_DIM
LAT_DECAY, LAT_AAA, LAT_GATE = 128, 128, 256
LAT_COLS = LAT_DECAY + LAT_AAA + LAT_GATE


def _dot(a, b):
    return jnp.dot(a, b, preferred_element_type=F32)


def _dot_nt(a, b):
    return lax.dot_general(a, b, (((1,), (1,)), ((), ())), preferred_element_type=F32)


def _dot_tn(a, b):
    return lax.dot_general(a, b, (((0,), (0,)), ((), ())), preferred_element_type=F32)


def _rms(x, gain):
    ms = jnp.mean(x * x, axis=-1, keepdims=True)
    return x * lax.rsqrt(ms + NORM_EPS) * gain


def _sigmoid(x):
    return 1.0 / (1.0 + jnp.exp(-x))


def _params(semantics):
    return pltpu.CompilerParams(dimension_semantics=semantics, vmem_limit_bytes=VMEM_LIMIT_BYTES)


def _ffn_kernel(x_ref, gpre_ref, gpost_ref, wg_ref, wu_ref, wd_ref, o_ref, xn_ref):
    j = pl.program_id(1)

    @pl.when(j == 0)
    def _():
        xn_ref[...] = _rms(x_ref[...], gpre_ref[...]).astype(BF16)
        o_ref[...] = jnp.zeros_like(o_ref)

    xn = xn_ref[...]
    gate = _dot(xn, wg_ref[...])
    up = _dot(xn, wu_ref[...])
    act = (gate * _sigmoid(gate) * up).astype(BF16)
    o_ref[...] += _dot(act, wd_ref[...])

    @pl.when(j == pl.num_programs(1) - 1)
    def _():
        o_ref[...] = x_ref[...] + MACARON_WEIGHT * _rms(o_ref[...], gpost_ref[...])


def _ffn(x, g_pre, g_post, w_gate, w_up, w_down, *, tm=512, tf=512):
    t, d = x.shape
    f = w_gate.shape[1]
    return pl.pallas_call(
        _ffn_kernel,
        grid=(t // tm, f // tf),
        in_specs=[
            pl.BlockSpec((tm, d), lambda i, j: (i, 0)),
            pl.BlockSpec((1, d), lambda i, j: (0, 0)),
            pl.BlockSpec((1, d), lambda i, j: (0, 0)),
            pl.BlockSpec((d, tf), lambda i, j: (0, j)),
            pl.BlockSpec((d, tf), lambda i, j: (0, j)),
            pl.BlockSpec((tf, d), lambda i, j: (j, 0)),
        ],
        out_specs=pl.BlockSpec((tm, d), lambda i, j: (i, 0)),
        out_shape=jax.ShapeDtypeStruct((t, d), F32),
        scratch_shapes=[pltpu.VMEM((tm, d), BF16)],
        compiler_params=_params(("parallel", "arbitrary")),
        name="ffn",
    )(x, g_pre, g_post, w_gate, w_up, w_down)


def _inproj_kernel(h_ref, g_ref, w_ref, o_ref, u_ref):
    @pl.when(pl.program_id(1) == 0)
    def _():
        u_ref[...] = _rms(h_ref[...], g_ref[...]).astype(BF16)

    o_ref[...] = _dot(u_ref[...], w_ref[...])


def _inproj(h, gain, w, *, tm=1024, tn=512):
    t, d = h.shape
    n = w.shape[1]
    return pl.pallas_call(
        _inproj_kernel,
        grid=(t // tm, n // tn),
        in_specs=[
            pl.BlockSpec((tm, d), lambda i, j: (i, 0)),
            pl.BlockSpec((1, d), lambda i, j: (0, 0)),
            pl.BlockSpec((d, tn), lambda i, j: (0, j)),
        ],
        out_specs=pl.BlockSpec((tm, tn), lambda i, j: (i, j)),
        out_shape=jax.ShapeDtypeStruct((t, n), F32),
        scratch_shapes=[pltpu.VMEM((tm, d), BF16)],
        compiler_params=_params(("parallel", "arbitrary")),
        name="inproj",
    )(h, gain, w)


def _token_shift_lerp(z, prev_row, mu):
    rolled = pltpu.roll(z, 1, 0)
    row = lax.broadcasted_iota(jnp.int32, z.shape, 0)
    shifted = jnp.where(row == 0, prev_row, rolled)
    return z + (shifted - z) * mu


def _odd_blocks(x, s):
    return jnp.concatenate([x[b * s:(b + 1) * s] for b in range(1, x.shape[0] // s, 2)], axis=0)


def _spread_odd_blocks(u, s):
    zero = jnp.zeros((s, u.shape[1]), u.dtype)
    parts = []
    for b in range(u.shape[0] // s):
        parts += [zero, u[b * s:(b + 1) * s]]
    return jnp.concatenate(parts, axis=0)


def _rwkv_kernel(zr_ref, zl_ref, mur_ref, mul_ref, w0_ref, w2_ref, a0_ref, a2_ref, g2_ref,
                 kk_ref, ka_ref, rk_ref, gnw_ref, gnb_ref, hblk_ref,
                 o_ref,
                 prev_r_ref, prev_l_ref, state_ref, ops_s, gend_s, bonus_s, gate_s, *, width, ct):
    i = pl.program_id(1)
    n_pairs = width // PAIR
    span = SUBCHUNKS * CHUNK
    n_trips = ct // span
    units = [(sub, p) for sub in range(SUBCHUNKS) for p in range(n_pairs)]
    c2 = 2 * CHUNK
    OP_AT, OP_RT, OP_BT, OP_KT, OP_V, OP_BH, OP_KH = range(7)

    @pl.when(i == 0)
    def _():
        prev_r_ref[...] = jnp.zeros_like(prev_r_ref)
        prev_l_ref[...] = jnp.zeros_like(prev_l_ref)
        state_ref[...] = jnp.zeros_like(state_ref)

    hblk = hblk_ref[...]

    def head_sum(x):
        xb = x.astype(BF16)
        return jnp.concatenate(
            [_dot(xb[:, q * MXU_DIM:(q + 1) * MXU_DIM], hblk) for q in range(width // MXU_DIM)], axis=1)

    ri = lax.broadcasted_iota(jnp.int32, (c2, c2), 0)
    ci = lax.broadcasted_iota(jnp.int32, (c2, c2), 1)
    same_head = (ri >> LOG_CHUNK) == (ci >> LOG_CHUNK)
    strict = same_head & (ri > ci)
    incl = same_head & (ri >= ci)
    eye = (ri == ci).astype(F32)

    def level_mask(rows, s):
        cols = lax.broadcasted_iota(jnp.int32, rows.shape, 1)
        sh = s.bit_length() - 1
        return ((rows >> (sh + 1)) == (cols >> (sh + 1))) & (((rows >> sh) & 1) == 1) & (((cols >> sh) & 1) == 0)

    levels = []
    s = 1
    while s < CHUNK:
        levels.append((s, level_mask(ri, s) if s < SUBLANES else level_mask(_odd_blocks(ri, s), s)))
        s *= 2
    tri_r = lax.broadcasted_iota(jnp.int32, (span, span), 0)
    tri_c = lax.broadcasted_iota(jnp.int32, (span, span), 1)
    cum_tri = ((tri_r >= tri_c) & ((tri_r >> LOG_CHUNK) == (tri_c >> LOG_CHUNK))).astype(BF16)
    lane = lax.broadcasted_iota(jnp.int32, (CHUNK, PAIR), 1)
    lo = lane < HEAD_DIM
    diag_pp = lax.broadcasted_iota(jnp.int32, (PAIR, PAIR), 0) == lax.broadcasted_iota(jnp.int32, (PAIR, PAIR), 1)
    zeros_pp = jnp.zeros((c2, PAIR), BF16)

    def stack(x):
        return jnp.concatenate([jnp.where(lo, x, 0.0), jnp.where(lo, 0.0, x)], axis=0).astype(BF16)

    def prepare(rows, prev_r, prev_l):
        zr = zr_ref[rows, :]
        zl = zl_ref[rows, :]
        zsr = _token_shift_lerp(zr, prev_r, mur_ref[...])
        zsl = _token_shift_lerp(zl, prev_l, mul_ref[...])
        r = zsr[:, 0:width]
        k = zsr[:, width:2 * width]
        v = zsr[:, 2 * width:3 * width]
        lw = zsl[:, 0:LAT_DECAY]
        la = zsl[:, LAT_DECAY:LAT_DECAY + LAT_AAA]
        lg = zsl[:, LAT_DECAY + LAT_AAA:LAT_COLS]
        yield

        ld = DECAY_SCALE * _sigmoid(w0_ref[...] + _dot(jnp.tanh(lw).astype(BF16), w2_ref[...]))
        a = _sigmoid(a0_ref[...] + _dot(la.astype(BF16), a2_ref[...]))
        gate_s[...] = _dot(_sigmoid(lg).astype(BF16), g2_ref[...])
        yield
        kk = k * kk_ref[...]
        kk = kk * lax.rsqrt(jnp.maximum(head_sum(kk * kk), L2_EPS * L2_EPS))
        k2 = k * (1.0 + (a - 1.0) * ka_ref[...])
        bonus_s[...] = head_sum(r * k2 * rk_ref[...]) * v
        bvec = kk * a
        yield

        l1 = ld.astype(BF16)
        l2f = ld - l1.astype(F32)
        l2 = l2f.astype(BF16)
        l3 = (l2f - l2.astype(F32)).astype(BF16)
        gcum = _dot(cum_tri, l1) + _dot(cum_tri, l2) + _dot(cum_tri, l3)
        gtots = [gcum[(sub + 1) * CHUNK - 1:(sub + 1) * CHUNK, :] for sub in range(SUBCHUNKS)]
        gtot = jnp.concatenate([jnp.broadcast_to(x, (CHUNK, width)) for x in gtots], axis=0)
        e_neg = jnp.exp(-gcum)
        e_end = jnp.exp(gtot - gcum)
        gend_s[...] = jnp.exp(jnp.concatenate(gtots, axis=0))
        full = {OP_AT: -kk * jnp.exp(gcum - ld), OP_RT: r * jnp.exp(gcum), OP_BT: bvec * e_neg, OP_KT: k2 * e_neg,
                OP_V: v, OP_BH: bvec * e_end, OP_KH: k2 * e_end}
        yield
        for op, x in full.items():
            for u, (sub, p) in enumerate(units):
                ops_s[op, u] = stack(x[sub * CHUNK:(sub + 1) * CHUNK, p * PAIR:(p + 1) * PAIR])
            if op % 2 == 1:
                yield

    def solve(rows):
        pairs = range(len(units))
        at, rt, bt, kt, vs, bh, kh = [[ops_s[op, p] for p in pairs] for op in range(7)]
        g_end = gend_s[...]
        bonus = bonus_s[...]
        gate = gate_s[...]

        scores = [_dot_nt(jnp.concatenate([at[p], rt[p]], axis=0), jnp.concatenate([bt[p], kt[p]], axis=0)) for p in pairs]
        yield
        a_ab = [jnp.where(strict, sc[0:c2, 0:c2], 0.0) for sc in scores]
        a_ak = [jnp.where(strict, sc[0:c2, c2:2 * c2], 0.0).astype(BF16) for sc in scores]
        a_r = [jnp.concatenate([jnp.where(incl, sc[c2:2 * c2, 0:c2], 0.0), jnp.where(incl, sc[c2:2 * c2, c2:2 * c2], 0.0)],
                               axis=1).astype(BF16) for sc in scores]

        tinv = [eye + jnp.where(levels[0][1], x, 0.0) for x in a_ab]
        yield
        for s, m in levels[1:]:
            tb = [t.astype(BF16) for t in tinv]
            if s < SUBLANES:
                inner = [_dot(jnp.where(m, a_ab[p], 0.0).astype(BF16), tb[p]).astype(BF16) for p in pairs]
                tinv = [tinv[p] + _dot(tb[p], inner[p]) for p in pairs]
            else:
                inner = [_dot(jnp.where(m, _odd_blocks(a_ab[p], s), 0.0).astype(BF16), tb[p]) for p in pairs]
                inner = [_spread_odd_blocks(x, s).astype(BF16) for x in inner]
                upd = [_dot(_odd_blocks(tinv[p], s).astype(BF16), inner[p]) for p in pairs]
                tinv = [tinv[p] + _spread_odd_blocks(upd[p], s) for p in pairs]
            yield
        tb = [t.astype(BF16) for t in tinv]

        akv = [_dot(a_ak[p], vs[p]).astype(BF16) for p in pairs]
        wub = [_dot(tb[p], jnp.concatenate([at[p], akv[p]], axis=1)).astype(BF16) for p in pairs]
        rhs = [jnp.concatenate([wub[p], jnp.concatenate([zeros_pp, vs[p]], axis=1)], axis=0) for p in pairs]
        qy = [_dot(a_r[p], rhs[p]) for p in pairs]
        pz = [_dot_tn(jnp.concatenate([bh[p], kh[p]], axis=0), rhs[p]) for p in pairs]
        yield
        lhs = []
        for u, (sub, p) in enumerate(units):
            qm = rt[u].astype(F32) + qy[u][:, 0:PAIR]
            p_mat = jnp.where(diag_pp, g_end[sub:sub + 1, p * PAIR:(p + 1) * PAIR], 0.0) + pz[u][:, 0:PAIR]
            lhs.append(jnp.concatenate([qm[0:CHUNK, :] + qm[CHUNK:c2, :], p_mat], axis=0).astype(BF16))
        state = [state_ref[p] for p in range(n_pairs)]
        ys = []
        for u, (sub, p) in enumerate(units):
            yv = qy[u][:, PAIR:2 * PAIR]
            both = _dot(lhs[u], state[p].astype(BF16))
            ys.append(both[0:CHUNK, :] + yv[0:CHUNK, :] + yv[CHUNK:c2, :])
            state[p] = both[CHUNK:CHUNK + PAIR, :] + pz[u][:, PAIR:2 * PAIR]
        for p in range(n_pairs):
            state_ref[p] = state[p]
        y = jnp.concatenate([jnp.concatenate(ys[sub * n_pairs:(sub + 1) * n_pairs], axis=1) for sub in range(SUBCHUNKS)], axis=0)

        d = y - head_sum(y) * (1.0 / HEAD_DIM)
        var = head_sum(d * d) * (1.0 / HEAD_DIM)
        yn = d * lax.rsqrt(var + GN_EPS) * gnw_ref[...] + gnb_ref[...]
        o_ref[rows, :] = ((yn + bonus) * gate).astype(o_ref.dtype)
        yield

    def chunk_rows(c):
        return pl.ds(pl.multiple_of(c * span, span), span)

    for _ in prepare(pl.ds(0, span), prev_r_ref[...], prev_l_ref[...]):
        pass
    prev_r_ref[...] = zr_ref[ct - 1:ct, :]
    prev_l_ref[...] = zl_ref[ct - 1:ct, :]

    def chunk_body(c, carry):
        stages = solve(chunk_rows(c))
        next(stages)
        nxt = jnp.minimum(c + 1, n_trips - 1)
        tail = pl.ds(pl.multiple_of(nxt * span - SUBLANES, SUBLANES), SUBLANES)
        pieces = prepare(chunk_rows(nxt), zr_ref[tail, :][SUBLANES - 1:SUBLANES, :], zl_ref[tail, :][SUBLANES - 1:SUBLANES, :])
        for _ in itertools.zip_longest(pieces, stages):
            pass
        return carry

    lax.fori_loop(0, n_trips, chunk_body, 0)


def _rwkv(p, batch, seq, lat_block, mu_r, mu_l, w0, w2p, a0, a2p, g2p, k_k, k_a, r_k, gn_w, gn_b, *, ct=512):
    width = w0.shape[1]
    n_heads = width // HEAD_DIM
    steps = seq // ct
    assert ct // (SUBCHUNKS * CHUNK) >= 2
    blk = jnp.arange(MXU_DIM, dtype=jnp.int32) // HEAD_DIM
    hblk = (blk[:, None] == blk[None, :]).astype(BF16)
    row = lambda shape: pl.BlockSpec(shape, lambda b, i: (0, 0))
    kernel = functools.partial(_rwkv_kernel, width=width, ct=ct)
    return pl.pallas_call(
        kernel,
        grid=(batch, steps),
        in_specs=[
            pl.BlockSpec((ct, 3 * width), lambda b, i: (b * steps + i, 0)),
            pl.BlockSpec((ct, LAT_COLS), lambda b, i: (b * steps + i, lat_block)),
            row((1, 3 * width)), row((1, LAT_COLS)),
            row((1, width)), row((LAT_DECAY, width)),
            row((1, width)), row((LAT_AAA, width)),
            row((LAT_GATE, width)),
            row((1, width)), row((1, width)), row((1, width)), row((1, width)), row((1, width)),
            row((MXU_DIM, MXU_DIM)),
        ],
        out_specs=pl.BlockSpec((ct, width), lambda b, i: (b * steps + i, 0)),
        out_shape=jax.ShapeDtypeStruct((batch * seq, width), BF16),
        scratch_shapes=[
            pltpu.VMEM((1, 3 * width), F32), pltpu.VMEM((1, LAT_COLS), F32),
            pltpu.VMEM((n_heads // 2, PAIR, PAIR), F32),
            pltpu.VMEM((7, SUBCHUNKS * n_heads // 2, 2 * CHUNK, PAIR), BF16),
            pltpu.VMEM((SUBCHUNKS, width), F32), pltpu.VMEM((SUBCHUNKS * CHUNK, width), F32), pltpu.VMEM((SUBCHUNKS * CHUNK, width), F32),
        ],
        compiler_params=_params(("arbitrary", "arbitrary")),
        name="rwkv",
    )(p, p, mu_r, mu_l, w0, w2p, a0, a2p, g2p, k_k, k_a, r_k, gn_w, gn_b, hblk)


def _pool_kernel(z_ref, w_ref, scale_ref, o_ref, halo_ref, *, tp, group_dim):
    i = pl.program_id(1)

    @pl.when(i == 0)
    def _():
        halo_ref[...] = jnp.zeros_like(halo_ref)

    z = z_ref[...]
    ext = jnp.concatenate([halo_ref[...], z], axis=0)
    halo_ref[...] = z[tp - POOL_HALO:tp, :]
    pos = (i * tp + lax.broadcasted_iota(jnp.int32, (tp, group_dim), 0) + 1).astype(F32)

    run = ext
    span = 1
    for gi, win in enumerate(POOL_WINDOWS):
        cols = slice(gi * group_dim, (gi + 1) * group_dim)
        while span < win:
            run = run + pltpu.roll(run, span, 0)
            span *= 2
        total = run[POOL_HALO:, cols]
        pooled = total / jnp.minimum(pos, float(win))
        mixed = (pooled - z[:, cols]).astype(BF16)
        o_ref[:, cols] = (_dot(mixed, w_ref[gi]) * scale_ref[:, cols]).astype(o_ref.dtype)


def _pool(p, batch, seq, col_block, pool_w, pool_scale, *, tp=512):
    groups, group_dim, _ = pool_w.shape
    width = groups * group_dim
    steps = seq // tp
    kernel = functools.partial(_pool_kernel, tp=tp, group_dim=group_dim)
    return pl.pallas_call(
        kernel,
        grid=(batch, steps),
        in_specs=[
            pl.BlockSpec((tp, width), lambda b, i: (b * steps + i, col_block)),
            pl.BlockSpec((groups, group_dim, group_dim), lambda b, i: (0, 0, 0)),
            pl.BlockSpec((1, width), lambda b, i: (0, 0)),
        ],
        out_specs=pl.BlockSpec((tp, width), lambda b, i: (b * steps + i, 0)),
        out_shape=jax.ShapeDtypeStruct((batch * seq, width), BF16),
        scratch_shapes=[pltpu.VMEM((POOL_HALO, width), F32)],
        compiler_params=_params(("arbitrary", "arbitrary")),
        name="pool",
    )(p, pool_w, pool_scale)


def _mix_kernel(ya_ref, yb_ref, ga_ref, gb_ref, h_ref, wa_ref, wb_ref, wo_ref, gpost_ref, o_ref):
    pa = _dot(ya_ref[...], wa_ref[...])
    pb = _dot(yb_ref[...], wb_ref[...])
    m = (_sigmoid(ga_ref[...]) * pa + _sigmoid(gb_ref[...]) * pb).astype(BF16)
    mx = _dot(m, wo_ref[...])
    o_ref[...] = h_ref[...] + _rms(mx, gpost_ref[...])


def _mix_out(ya, yb, p, ga_block, gb_block, h, wa, wb, wo, g_post, *, tm=256):
    t, d = h.shape
    wa_rows, wb_rows = wa.shape[0], wb.shape[0]
    const = lambda shape: pl.BlockSpec(shape, lambda i: (0, 0), pipeline_mode=pl.Buffered(1))
    return pl.pallas_call(
        _mix_kernel,
        grid=(t // tm,),
        in_specs=[
            pl.BlockSpec((tm, wa_rows), lambda i: (i, 0)),
            pl.BlockSpec((tm, wb_rows), lambda i: (i, 0)),
            pl.BlockSpec((tm, d), lambda i: (i, ga_block)),
            pl.BlockSpec((tm, d), lambda i: (i, gb_block)),
            pl.BlockSpec((tm, d), lambda i: (i, 0)),
            const((wa_rows, d)), const((wb_rows, d)), const((d, d)),
            pl.BlockSpec((1, d), lambda i: (0, 0)),
        ],
        out_specs=pl.BlockSpec((tm, d), lambda i: (i, 0)),
        out_shape=jax.ShapeDtypeStruct((t, d), F32),
        compiler_params=_params(("parallel",)),
        name="mix_out",
    )(ya, yb, p, p, h, wa, wb, wo, g_post)


def _pad_cols(w, n):
    return jnp.pad(w, ((0, 0), (0, n - w.shape[1])))


def _pad_rows(w, n):
    return jnp.pad(w, ((0, n - w.shape[0]), (0, 0)))


def kernel(x, ln_ffn1_pre, ln_ffn1_post, ffn1_gate, ffn1_up, ffn1_down, ln_mix_pre, ln_mix_post, w_in, rwkv_mu, rwkv_w0, rwkv_w2, rwkv_a0, rwkv_a2, rwkv_g2, rwkv_k_k, rwkv_k_a, rwkv_r_k, rwkv_gn_w, rwkv_gn_b, w_proj_a, pool_w, pool_scale, w_proj_b, w_out, ln_ffn2_pre, ln_ffn2_post, ffn2_gate, ffn2_up, ffn2_down):
    batch, seq, d = x.shape
    depth = w_in.shape[0]
    width = rwkv_w0.shape[1]
    pool_width = pool_scale.shape[1]
    n_decay, n_aaa, n_gate = rwkv_w2.shape[1], rwkv_a2.shape[1], rwkv_g2.shape[1]
    rkv = 3 * width
    rwkv_cols = rkv + n_decay + n_aaa + n_gate
    assert n_decay <= LAT_DECAY and n_aaa <= LAT_AAA and n_gate <= LAT_GATE
    assert rkv % pool_width == 0 and (rkv + pool_width) % d == 0 and (rkv + pool_width + 2 * d) % LAT_COLS == 0

    h = x.reshape(batch * seq, d)
    row = lambda v: v.reshape(1, -1)
    for l in range(depth):
        h = _ffn(h, row(ln_ffn1_pre[l]), row(ln_ffn1_post[l]),
                 ffn1_gate[l].astype(BF16), ffn1_up[l].astype(BF16), ffn1_down[l].astype(BF16))

        wl = w_in[l]
        o1, o2 = rkv + n_decay, rkv + n_decay + n_aaa
        w_cat = jnp.concatenate([
            wl[:, :rkv], wl[:, rwkv_cols:],
            _pad_cols(wl[:, rkv:o1], LAT_DECAY), _pad_cols(wl[:, o1:o2], LAT_AAA), _pad_cols(wl[:, o2:rwkv_cols], LAT_GATE),
        ], axis=1).astype(BF16)
        mu = rwkv_mu[l]
        mu_l = jnp.concatenate([
            jnp.pad(mu[rkv:o1], (0, LAT_DECAY - n_decay)), jnp.pad(mu[o1:o2], (0, LAT_AAA - n_aaa)),
            jnp.pad(mu[o2:], (0, LAT_GATE - n_gate))])
        p = _inproj(h, row(ln_mix_pre[l]), w_cat)

        pool_block = rkv // pool_width
        ga_block = (rkv + pool_width) // d
        lat_block = (rkv + pool_width + 2 * d) // LAT_COLS
        ya = _rwkv(p, batch, seq, lat_block, row(mu[:rkv]), row(mu_l),
                   row(rwkv_w0[l]), _pad_rows(rwkv_w2[l], LAT_DECAY).astype(BF16),
                   row(rwkv_a0[l]), _pad_rows(rwkv_a2[l], LAT_AAA).astype(BF16),
                   _pad_rows(rwkv_g2[l], LAT_GATE).astype(BF16),
                   row(rwkv_k_k[l]), row(rwkv_k_a[l]), row(rwkv_r_k[l]), row(rwkv_gn_w[l]), row(rwkv_gn_b[l]))
        yb = _pool(p, batch, seq, pool_block, pool_w[l].astype(BF16), row(pool_scale[l]))
        h = _mix_out(ya, yb, p, ga_block, ga_block + 1, h,
                     w_proj_a[l].astype(BF16), w_proj_b[l].astype(BF16), w_out[l].astype(BF16), row(ln_mix_post[l]))

        h = _ffn(h, row(ln_ffn2_pre[l]), row(ln_ffn2_post[l]),
                 ffn2_gate[l].astype(BF16), ffn2_up[l].astype(BF16), ffn2_down[l].astype(BF16))
    return h.reshape(batch, seq, d)
```

```python
import functools
import itertools

import jax
import jax.numpy as jnp
from jax import lax
from jax.experimental import pallas as pl
from jax.experimental.pallas import tpu as pltpu

F32 = jnp.float32
BF16 = jnp.bfloat16

LANES = 128
SUBLANES = 8
MXU_DIM = 256
VMEM_LIMIT_BYTES = 56 * 1024 * 1024

NORM_EPS = 1e-6
GN_EPS = 64e-5
L2_EPS = 1e-12
DECAY_SCALE = -0.6065306597126334
MACARON_WEIGHT = 0.5

HEAD_DIM = 64
POOL_WINDOWS = (2, 4, 8, 16)
POOL_HALO = 16
CHUNK = 64
LOG_CHUNK = 6
SUBCHUNKS = 2
PAIR = 2 * HEAD_DIM
LAT_DECAY, LAT_AAA, LAT_GATE = 128, 128, 256
LAT_COLS = LAT_DECAY + LAT_AAA + LAT_GATE


def _dot(a, b):
    return jnp.dot(a, b, preferred_element_type=F32)


def _dot_nt(a, b):
    return lax.dot_general(a, b, (((1,), (1,)), ((), ())), preferred_element_type=F32)


def _dot_tn(a, b):
    return lax.dot_general(a, b, (((0,), (0,)), ((), ())), preferred_element_type=F32)


def _rms(x, gain):
    ms = jnp.mean(x * x, axis=-1, keepdims=True)
    return x * lax.rsqrt(ms + NORM_EPS) * gain


def _sigmoid(x):
    return 1.0 / (1.0 + jnp.exp(-x))


def _params(semantics):
    return pltpu.CompilerParams(dimension_semantics=semantics, vmem_limit_bytes=VMEM_LIMIT_BYTES)


def _ffn_kernel(x_ref, gpre_ref, gpost_ref, wg_ref, wu_ref, wd_ref, o_ref, xn_ref, *, last_cols):
    j = pl.program_id(1)
    last = pl.num_programs(1) - 1

    @pl.when(j == 0)
    def _():
        xn_ref[...] = _rms(x_ref[...], gpre_ref[...]).astype(BF16)
        o_ref[...] = jnp.zeros_like(o_ref)

    def accumulate(cols):
        xn = xn_ref[...]
        gate = _dot(xn, wg_ref[:, 0:cols])
        up = _dot(xn, wu_ref[:, 0:cols])
        act = (gate * _sigmoid(gate) * up).astype(BF16)
        o_ref[...] += _dot(act, wd_ref[0:cols, :])

    @pl.when(j < last)
    def _():
        accumulate(wg_ref.shape[1])

    @pl.when(j == last)
    def _():
        accumulate(last_cols)
        o_ref[...] = x_ref[...] + MACARON_WEIGHT * _rms(o_ref[...], gpost_ref[...])


def _ffn(x, g_pre, g_post, w_gate, w_up, w_down, *, tm=512, tf=1024):
    t, d = x.shape
    f = w_gate.shape[1]
    steps = pl.cdiv(f, tf)
    kernel = functools.partial(_ffn_kernel, last_cols=f - (steps - 1) * tf)
    return pl.pallas_call(
        kernel,
        grid=(t // tm, steps),
        in_specs=[
            pl.BlockSpec((tm, d), lambda i, j: (i, 0)),
            pl.BlockSpec((1, d), lambda i, j: (0, 0)),
            pl.BlockSpec((1, d), lambda i, j: (0, 0)),
            pl.BlockSpec((d, tf), lambda i, j: (0, j)),
            pl.BlockSpec((d, tf), lambda i, j: (0, j)),
            pl.BlockSpec((tf, d), lambda i, j: (j, 0)),
        ],
        out_specs=pl.BlockSpec((tm, d), lambda i, j: (i, 0)),
        out_shape=jax.ShapeDtypeStruct((t, d), F32),
        scratch_shapes=[pltpu.VMEM((tm, d), BF16)],
        compiler_params=_params(("parallel", "arbitrary")),
        name="ffn",
    )(x, g_pre, g_post, w_gate, w_up, w_down)


def _token_shift_lerp(z, prev_row, mu):
    rolled = pltpu.roll(z, 1, 0)
    row = lax.broadcasted_iota(jnp.int32, z.shape, 0)
    shifted = jnp.where(row == 0, prev_row, rolled)
    return z + (shifted - z) * mu


def _inproj_kernel(h_ref, g_ref, w_ref, mu_ref, o_ref, u_ref, prev_ref, *, tiles_per_seq, last_cols):
    i = pl.program_id(0)
    j = pl.program_id(1)
    last = pl.num_programs(1) - 1

    @pl.when(j == 0)
    def _():
        u_ref[...] = _rms(h_ref[...], g_ref[...]).astype(BF16)

    def project(cols):
        z = _dot(u_ref[...], w_ref[:, 0:cols])
        first_of_seq = (i % tiles_per_seq) == 0
        prev = jnp.where(first_of_seq, 0.0, prev_ref[j, :, 0:cols])
        prev_ref[j, :, 0:cols] = z[z.shape[0] - 1:z.shape[0], :]
        o_ref[:, 0:cols] = _token_shift_lerp(z, prev, mu_ref[:, 0:cols]).astype(o_ref.dtype)

    @pl.when(j < last)
    def _():
        project(w_ref.shape[1])

    @pl.when(j == last)
    def _():
        project(last_cols)


def _inproj(h, gain, w, mu, seq, *, tm=1024, tn=1024):
    t, d = h.shape
    n = w.shape[1]
    assert seq % tm == 0
    steps = pl.cdiv(n, tn)
    kernel = functools.partial(_inproj_kernel, tiles_per_seq=seq // tm, last_cols=n - (steps - 1) * tn)
    return pl.pallas_call(
        kernel,
        grid=(t // tm, steps),
        in_specs=[
            pl.BlockSpec((tm, d), lambda i, j: (i, 0)),
            pl.BlockSpec((1, d), lambda i, j: (0, 0)),
            pl.BlockSpec((d, tn), lambda i, j: (0, j)),
            pl.BlockSpec((1, tn), lambda i, j: (0, j)),
        ],
        out_specs=pl.BlockSpec((tm, tn), lambda i, j: (i, j)),
        out_shape=jax.ShapeDtypeStruct((t, n), BF16),
        scratch_shapes=[pltpu.VMEM((tm, d), BF16), pltpu.VMEM((steps, 1, tn), F32)],
        compiler_params=_params(("arbitrary", "arbitrary")),
        name="inproj",
    )(h, gain, w, mu)


def _odd_blocks(x, s):
    return jnp.concatenate([x[b * s:(b + 1) * s] for b in range(1, x.shape[0] // s, 2)], axis=0)


def _spread_odd_blocks(u, s):
    zero = jnp.zeros((s, u.shape[1]), u.dtype)
    parts = []
    for b in range(u.shape[0] // s):
        parts += [zero, u[b * s:(b + 1) * s]]
    return jnp.concatenate(parts, axis=0)


def _rwkv_kernel(zr_ref, zl_ref, w0_ref, w2_ref, a0_ref, a2_ref, g2_ref,
                 kk_ref, ka_ref, rk_ref, gnw_ref, gnb_ref, hblk_ref,
                 o_ref,
                 state_ref, nat_s, stk_s, gend_s, bonus_s, gate_s, *, width, ct):
    i = pl.program_id(1)
    n_pairs = width // PAIR
    span = SUBCHUNKS * CHUNK
    n_trips = ct // span
    units = [(sub, p) for sub in range(SUBCHUNKS) for p in range(n_pairs)]
    c2 = 2 * CHUNK
    OP_AT, OP_RT, OP_BT, OP_KT, OP_V, OP_BH, OP_KH = range(7)
    NATURAL = (OP_AT, OP_RT, OP_V, OP_BH, OP_KH)
    STACKED = (OP_AT, OP_BT, OP_KT, OP_V)

    @pl.when(i == 0)
    def _():
        state_ref[...] = jnp.zeros_like(state_ref)

    hblk = hblk_ref[...]

    def head_sum(x):
        xb = x.astype(BF16)
        return jnp.concatenate(
            [_dot(xb[:, q * MXU_DIM:(q + 1) * MXU_DIM], hblk) for q in range(width // MXU_DIM)], axis=1)

    ti = lax.broadcasted_iota(jnp.int32, (CHUNK, PAIR), 0)
    si = lax.broadcasted_iota(jnp.int32, (CHUNK, PAIR), 1) & (CHUNK - 1)
    strict = ti > si
    incl = ti >= si
    eye = (ti == si).astype(F32)

    def level_mask(rows, s):
        cols = lax.broadcasted_iota(jnp.int32, rows.shape, 1) & (CHUNK - 1)
        sh = s.bit_length() - 1
        return ((rows >> (sh + 1)) == (cols >> (sh + 1))) & (((rows >> sh) & 1) == 1) & (((cols >> sh) & 1) == 0)

    levels = []
    s = 1
    while s < CHUNK:
        levels.append((s, level_mask(ti, s) if s < SUBLANES else level_mask(_odd_blocks(ti, s), s)))
        s *= 2
    tri_r = lax.broadcasted_iota(jnp.int32, (span, span), 0)
    tri_c = lax.broadcasted_iota(jnp.int32, (span, span), 1)
    cum_tri = ((tri_r >= tri_c) & ((tri_r >> LOG_CHUNK) == (tri_c >> LOG_CHUNK))).astype(BF16)
    head_masks = {}
    for n in (PAIR, 2 * PAIR):
        first = (lax.broadcasted_iota(jnp.int32, (CHUNK, n), 1) & HEAD_DIM) == 0
        head_masks[n] = (first.astype(BF16), (~first).astype(BF16))
    pr = lax.broadcasted_iota(jnp.int32, (PAIR, 2 * PAIR), 0)
    pc = lax.broadcasted_iota(jnp.int32, (PAIR, 2 * PAIR), 1)
    diag_pp = lax.broadcasted_iota(jnp.int32, (PAIR, PAIR), 0) == lax.broadcasted_iota(jnp.int32, (PAIR, PAIR), 1)
    same_head_pp = (pr >> LOG_CHUNK) == ((pc >> LOG_CHUNK) & 1)
    zeros_st = jnp.zeros((c2, PAIR), BF16)
    zeros_nt = jnp.zeros((CHUNK, PAIR), BF16)

    def stack(xb):
        first, second = head_masks[xb.shape[1]]
        return jnp.concatenate([xb * first, xb * second], axis=0)

    def prepare(rows):
        r = zr_ref[rows, 0:width].astype(F32)
        k = zr_ref[rows, width:2 * width].astype(F32)
        v = zr_ref[rows, 2 * width:3 * width].astype(F32)
        lw = zl_ref[rows, 0:LAT_DECAY].astype(F32)
        la = zl_ref[rows, LAT_DECAY:LAT_DECAY + LAT_AAA]
        lg = zl_ref[rows, LAT_DECAY + LAT_AAA:LAT_COLS].astype(F32)
        yield

        ld = DECAY_SCALE * _sigmoid(w0_ref[...] + _dot(jnp.tanh(lw).astype(BF16), w2_ref[...]))
        a = _sigmoid(a0_ref[...] + _dot(la, a2_ref[...]))
        gate_s[...] = _dot(_sigmoid(lg).astype(BF16), g2_ref[...])
        yield
        kk = k * kk_ref[...]
        kk = kk * lax.rsqrt(jnp.maximum(head_sum(kk * kk), L2_EPS * L2_EPS))
        k2 = k * (1.0 + (a - 1.0) * ka_ref[...])
        bonus_s[...] = head_sum(r * k2 * rk_ref[...]) * v
        bvec = kk * a
        yield

        l1 = ld.astype(BF16)
        l2 = (ld - l1.astype(F32)).astype(BF16)
        gcum = _dot(cum_tri, l1) + _dot(cum_tri, l2)
        gtots = [gcum[(sub + 1) * CHUNK - 1:(sub + 1) * CHUNK, :] for sub in range(SUBCHUNKS)]
        gtot = jnp.concatenate([jnp.broadcast_to(x, (CHUNK, width)) for x in gtots], axis=0)
        e_neg = jnp.exp(-gcum)
        e_end = jnp.exp(gtot - gcum)
        gend_s[...] = jnp.exp(jnp.concatenate(gtots, axis=0))
        full = {OP_AT: -kk * jnp.exp(gcum - ld), OP_RT: r * jnp.exp(gcum), OP_BT: bvec * e_neg, OP_KT: k2 * e_neg,
                OP_V: v, OP_BH: bvec * e_end, OP_KH: k2 * e_end}
        yield
        for op, x in full.items():
            for u, (sub, p) in enumerate(units):
                xb = x[sub * CHUNK:(sub + 1) * CHUNK, p * PAIR:(p + 1) * PAIR].astype(BF16)
                if op in NATURAL:
                    nat_s[NATURAL.index(op), u] = xb
                if op in STACKED:
                    stk_s[STACKED.index(op), u] = stack(xb)
            if op % 2 == 1:
                yield

    def solve(rows):
        pairs = range(len(units))
        at, rt, vn, bh, kh = [[nat_s[o, p] for p in pairs] for o in range(len(NATURAL))]
        at_s, bt_s, kt_s, v_s = [[stk_s[o, p] for p in pairs] for o in range(len(STACKED))]
        g_end = gend_s[...]
        bonus = bonus_s[...]
        gate = gate_s[...]

        scores = [_dot_nt(jnp.concatenate([at[p], rt[p]], axis=0), jnp.concatenate([bt_s[p], kt_s[p]], axis=0)) for p in pairs]
        yield
        a_ab = [jnp.where(strict, sc[0:CHUNK, 0:PAIR], 0.0) for sc in scores]
        a_ak = [jnp.where(strict, sc[0:CHUNK, PAIR:2 * PAIR], 0.0).astype(BF16) for sc in scores]
        a_r = [jnp.concatenate([jnp.where(incl, sc[CHUNK:c2, 0:PAIR], 0.0), jnp.where(incl, sc[CHUNK:c2, PAIR:2 * PAIR], 0.0)],
                               axis=1).astype(BF16) for sc in scores]

        tinv = [eye + jnp.where(levels[0][1], x, 0.0) for x in a_ab]
        yield
        for s, m in levels[1:]:
            tb = [t.astype(BF16) for t in tinv]
            tstk = [stack(t) for t in tb]
            if s < SUBLANES:
                inner = [_dot(jnp.where(m, a_ab[p], 0.0).astype(BF16), tstk[p]).astype(BF16) for p in pairs]
                tinv = [tinv[p] + _dot(tb[p], stack(inner[p])) for p in pairs]
            else:
                inner = [_dot(jnp.where(m, _odd_blocks(a_ab[p], s), 0.0).astype(BF16), tstk[p]) for p in pairs]
                inner = [stack(_spread_odd_blocks(x, s).astype(BF16)) for x in inner]
                upd = [_dot(_odd_blocks(tinv[p], s).astype(BF16), inner[p]) for p in pairs]
                tinv = [tinv[p] + _spread_odd_blocks(upd[p], s) for p in pairs]
            yield
        tb = [t.astype(BF16) for t in tinv]

        akv = [_dot(a_ak[p], v_s[p]).astype(BF16) for p in pairs]
        wub = [_dot(tb[p], jnp.concatenate([at_s[p], stack(akv[p])], axis=1)).astype(BF16) for p in pairs]
        qy = [_dot(a_r[p], jnp.concatenate([stack(wub[p]), jnp.concatenate([zeros_st, v_s[p]], axis=1)], axis=0))
              for p in pairs]
        pz = [_dot_tn(jnp.concatenate([bh[p], kh[p]], axis=0),
                      jnp.concatenate([wub[p], jnp.concatenate([zeros_nt, vn[p]], axis=1)], axis=0)) for p in pairs]
        pz = [jnp.where(same_head_pp, x, 0.0) for x in pz]
        yield
        lhs = []
        for u, (sub, p) in enumerate(units):
            qm = rt[u].astype(F32) + qy[u][:, 0:PAIR]
            p_mat = jnp.where(diag_pp, g_end[sub:sub + 1, p * PAIR:(p + 1) * PAIR], 0.0) + pz[u][:, 0:PAIR]
            lhs.append(jnp.concatenate([qm, p_mat], axis=0).astype(BF16))
        state = [state_ref[p] for p in range(n_pairs)]
        ys = []
        for u, (sub, p) in enumerate(units):
            yv = qy[u][:, PAIR:2 * PAIR]
            both = _dot(lhs[u], state[p].astype(BF16))
            ys.append(both[0:CHUNK, :] + yv)
            state[p] = both[CHUNK:CHUNK + PAIR, :] + pz[u][:, PAIR:2 * PAIR]
        for p in range(n_pairs):
            state_ref[p] = state[p]
        y = jnp.concatenate([jnp.concatenate(ys[sub * n_pairs:(sub + 1) * n_pairs], axis=1) for sub in range(SUBCHUNKS)], axis=0)

        d = y - head_sum(y) * (1.0 / HEAD_DIM)
        var = head_sum(d * d) * (1.0 / HEAD_DIM)
        yn = d * lax.rsqrt(var + GN_EPS) * gnw_ref[...] + gnb_ref[...]
        o_ref[rows, :] = ((yn + bonus) * gate).astype(o_ref.dtype)
        yield

    def chunk_rows(c):
        return pl.ds(pl.multiple_of(c * span, span), span)

    for _ in prepare(pl.ds(0, span)):
        pass

    def chunk_body(c, carry):
        stages = solve(chunk_rows(c))
        next(stages)
        nxt = jnp.minimum(c + 1, n_trips - 1)
        pieces = prepare(chunk_rows(nxt))
        for _ in itertools.zip_longest(pieces, stages):
            pass
        return carry

    lax.fori_loop(0, n_trips, chunk_body, 0)


def _rwkv(p, batch, seq, lat_block, w0, w2p, a0, a2p, g2p, k_k, k_a, r_k, gn_w, gn_b, *, ct=1024):
    width = w0.shape[1]
    n_heads = width // HEAD_DIM
    assert seq % ct == 0 and ct % (SUBCHUNKS * CHUNK) == 0
    steps = seq // ct
    blk = jnp.arange(MXU_DIM, dtype=jnp.int32) // HEAD_DIM
    hblk = (blk[:, None] == blk[None, :]).astype(BF16)
    row = lambda shape: pl.BlockSpec(shape, lambda b, i: (0, 0))
    kernel = functools.partial(_rwkv_kernel, width=width, ct=ct)
    return pl.pallas_call(
        kernel,
        grid=(batch, steps),
        in_specs=[
            pl.BlockSpec((ct, 3 * width), lambda b, i: (b * steps + i, 0)),
            pl.BlockSpec((ct, LAT_COLS), lambda b, i: (b * steps + i, lat_block)),
            row((1, width)), row((LAT_DECAY, width)),
            row((1, width)), row((LAT_AAA, width)),
            row((LAT_GATE, width)),
            row((1, width)), row((1, width)), row((1, width)), row((1, width)), row((1, width)),
            row((MXU_DIM, MXU_DIM)),
        ],
        out_specs=pl.BlockSpec((ct, width), lambda b, i: (b * steps + i, 0)),
        out_shape=jax.ShapeDtypeStruct((batch * seq, width), BF16),
        scratch_shapes=[
            pltpu.VMEM((n_heads // 2, PAIR, PAIR), F32),
            pltpu.VMEM((5, SUBCHUNKS * n_heads // 2, CHUNK, PAIR), BF16),
            pltpu.VMEM((4, SUBCHUNKS * n_heads // 2, 2 * CHUNK, PAIR), BF16),
            pltpu.VMEM((SUBCHUNKS, width), F32), pltpu.VMEM((SUBCHUNKS * CHUNK, width), F32), pltpu.VMEM((SUBCHUNKS * CHUNK, width), F32),
        ],
        compiler_params=_params(("arbitrary", "arbitrary")),
        name="rwkv",
    )(p, p, w0, w2p, a0, a2p, g2p, k_k, k_a, r_k, gn_w, gn_b, hblk)


def _pool_kernel(z_ref, w_ref, scale_ref, o_ref, halo_ref, *, tp, group_dim):
    i = pl.program_id(1)

    @pl.when(i == 0)
    def _():
        halo_ref[...] = jnp.zeros_like(halo_ref)

    z = z_ref[...].astype(F32)
    ext = jnp.concatenate([halo_ref[...], z], axis=0)
    halo_ref[...] = z[tp - POOL_HALO:tp, :]
    pos = (i * tp + lax.broadcasted_iota(jnp.int32, (tp, group_dim), 0) + 1).astype(F32)

    run = ext
    span = 1
    for gi, win in enumerate(POOL_WINDOWS):
        cols = slice(gi * group_dim, (gi + 1) * group_dim)
        while span < win:
            run = run + pltpu.roll(run, span, 0)
            span *= 2
        total = run[POOL_HALO:, cols]
        pooled = total / jnp.minimum(pos, float(win))
        mixed = (pooled - z[:, cols]).astype(BF16)
        o_ref[:, cols] = (_dot(mixed, w_ref[gi]) * scale_ref[:, cols]).astype(o_ref.dtype)


def _pool(p, batch, seq, col_block, pool_w, pool_scale, *, tp=512):
    groups, group_dim, _ = pool_w.shape
    width = groups * group_dim
    steps = seq // tp
    kernel = functools.partial(_pool_kernel, tp=tp, group_dim=group_dim)
    return pl.pallas_call(
        kernel,
        grid=(batch, steps),
        in_specs=[
            pl.BlockSpec((tp, width), lambda b, i: (b * steps + i, col_block)),
            pl.BlockSpec((groups, group_dim, group_dim), lambda b, i: (0, 0, 0)),
            pl.BlockSpec((1, width), lambda b, i: (0, 0)),
        ],
        out_specs=pl.BlockSpec((tp, width), lambda b, i: (b * steps + i, 0)),
        out_shape=jax.ShapeDtypeStruct((batch * seq, width), BF16),
        scratch_shapes=[pltpu.VMEM((POOL_HALO, width), F32)],
        compiler_params=_params(("arbitrary", "arbitrary")),
        name="pool",
    )(p, pool_w, pool_scale)


def _mix_kernel(ya_ref, yb_ref, ga_ref, gb_ref, h_ref, wa_ref, wb_ref, wo_ref, gpost_ref, o_ref):
    pa = _dot(ya_ref[...], wa_ref[...])
    pb = _dot(yb_ref[...], wb_ref[...])
    m = (_sigmoid(ga_ref[...].astype(F32)) * pa + _sigmoid(gb_ref[...].astype(F32)) * pb).astype(BF16)
    mx = _dot(m, wo_ref[...])
    o_ref[...] = h_ref[...] + _rms(mx, gpost_ref[...])


def _mix_out(ya, yb, p, ga_block, gb_block, h, wa, wb, wo, g_post, *, tm=512):
    t, d = h.shape
    wa_rows, wb_rows = wa.shape[0], wb.shape[0]
    const = lambda shape: pl.BlockSpec(shape, lambda i: (0, 0), pipeline_mode=pl.Buffered(1))
    return pl.pallas_call(
        _mix_kernel,
        grid=(t // tm,),
        in_specs=[
            pl.BlockSpec((tm, wa_rows), lambda i: (i, 0)),
            pl.BlockSpec((tm, wb_rows), lambda i: (i, 0)),
            pl.BlockSpec((tm, d), lambda i: (i, ga_block)),
            pl.BlockSpec((tm, d), lambda i: (i, gb_block)),
            pl.BlockSpec((tm, d), lambda i: (i, 0)),
            const((wa_rows, d)), const((wb_rows, d)), const((d, d)),
            pl.BlockSpec((1, d), lambda i: (0, 0)),
        ],
        out_specs=pl.BlockSpec((tm, d), lambda i: (i, 0)),
        out_shape=jax.ShapeDtypeStruct((t, d), F32),
        compiler_params=_params(("parallel",)),
        name="mix_out",
    )(ya, yb, p, p, h, wa, wb, wo, g_post)


def _pad_cols(w, n):
    return jnp.pad(w, ((0, 0), (0, n - w.shape[1])))


def _pad_rows(w, n):
    return jnp.pad(w, ((0, n - w.shape[0]), (0, 0)))


def kernel(x, ln_ffn1_pre, ln_ffn1_post, ffn1_gate, ffn1_up, ffn1_down, ln_mix_pre, ln_mix_post, w_in, rwkv_mu, rwkv_w0, rwkv_w2, rwkv_a0, rwkv_a2, rwkv_g2, rwkv_k_k, rwkv_k_a, rwkv_r_k, rwkv_gn_w, rwkv_gn_b, w_proj_a, pool_w, pool_scale, w_proj_b, w_out, ln_ffn2_pre, ln_ffn2_post, ffn2_gate, ffn2_up, ffn2_down):
    batch, seq, d = x.shape
    depth = w_in.shape[0]
    width = rwkv_w0.shape[1]
    pool_width = pool_scale.shape[1]
    n_decay, n_aaa, n_gate = rwkv_w2.shape[1], rwkv_a2.shape[1], rwkv_g2.shape[1]
    rkv = 3 * width
    rwkv_cols = rkv + n_decay + n_aaa + n_gate
    assert n_decay <= LAT_DECAY and n_aaa <= LAT_AAA and n_gate <= LAT_GATE
    assert rkv % pool_width == 0 and (rkv + pool_width) % d == 0 and (rkv + pool_width + 2 * d) % LAT_COLS == 0

    h = x.reshape(batch * seq, d)
    row = lambda v: v.reshape(1, -1)
    for l in range(depth):
        h = _ffn(h, row(ln_ffn1_pre[l]), row(ln_ffn1_post[l]),
                 ffn1_gate[l].astype(BF16), ffn1_up[l].astype(BF16), ffn1_down[l].astype(BF16))

        wl = w_in[l]
        o1, o2 = rkv + n_decay, rkv + n_decay + n_aaa
        w_cat = jnp.concatenate([
            wl[:, :rkv], wl[:, rwkv_cols:],
            _pad_cols(wl[:, rkv:o1], LAT_DECAY), _pad_cols(wl[:, o1:o2], LAT_AAA), _pad_cols(wl[:, o2:rwkv_cols], LAT_GATE),
        ], axis=1).astype(BF16)
        mu = rwkv_mu[l]
        mu_cat = jnp.concatenate([
            mu[:rkv], jnp.zeros((w_in.shape[2] - rwkv_cols,), mu.dtype),
            jnp.pad(mu[rkv:o1], (0, LAT_DECAY - n_decay)), jnp.pad(mu[o1:o2], (0, LAT_AAA - n_aaa)),
            jnp.pad(mu[o2:], (0, LAT_GATE - n_gate))])
        p = _inproj(h, row(ln_mix_pre[l]), w_cat, row(mu_cat), seq)

        pool_block = rkv // pool_width
        ga_block = (rkv + pool_width) // d
        lat_block = (rkv + pool_width + 2 * d) // LAT_COLS
        ya = _rwkv(p, batch, seq, lat_block,
                   row(rwkv_w0[l]), _pad_rows(rwkv_w2[l], LAT_DECAY).astype(BF16),
                   row(rwkv_a0[l]), _pad_rows(rwkv_a2[l], LAT_AAA).astype(BF16),
                   _pad_rows(rwkv_g2[l], LAT_GATE).astype(BF16),
                   row(rwkv_k_k[l]), row(rwkv_k_a[l]), row(rwkv_r_k[l]), row(rwkv_gn_w[l]), row(rwkv_gn_b[l]))
        yb = _pool(p, batch, seq, pool_block, pool_w[l].astype(BF16), row(pool_scale[l]))
        h = _mix_out(ya, yb, p, ga_block, ga_block + 1, h,
                     w_proj_a[l].astype(BF16), w_proj_b[l].astype(BF16), w_out[l].astype(BF16), row(ln_mix_post[l]))

        h = _ffn(h, row(ln_ffn2_pre[l]), row(ln_ffn2_post[l]),
                 ffn2_gate[l].astype(BF16), ffn2_up[l].astype(BF16), ffn2_down[l].astype(BF16))
    return h.reshape(batch, seq, d)
```

```python
import functools
import itertools

import jax
import jax.numpy as jnp
from jax import lax
from jax.experimental import pallas as pl
from jax.experimental.pallas import tpu as pltpu

F32 = jnp.float32
BF16 = jnp.bfloat16

LANES = 128
SUBLANES = 8
MXU_DIM = 256
VMEM_BYTES = 64 * 1024 * 1024
VMEM_LIMIT_BYTES = 56 * 1024 * 1024

NORM_EPS = 1e-6
GN_EPS = 64e-5
L2_EPS = 1e-12
DECAY_SCALE = -0.6065306597126334
MACARON_WEIGHT = 0.5

HEAD_DIM = 64
POOL_WINDOWS = (2, 4, 8, 16)
POOL_HALO = 16
CHUNK = 64
LOG_CHUNK = 6
SUBCHUNKS = 2
PAIR = 2 * HEAD_DIM
LAT_DECAY, LAT_AAA, LAT_GATE = 128, 128, 256
LAT_COLS = LAT_DECAY + LAT_AAA + LAT_GATE


def _dot(a, b):
    return jnp.dot(a, b, preferred_element_type=F32)


def _dot_nt(a, b):
    return lax.dot_general(a, b, (((1,), (1,)), ((), ())), preferred_element_type=F32)


def _dot_tn(a, b):
    return lax.dot_general(a, b, (((0,), (0,)), ((), ())), preferred_element_type=F32)


def _rms(x, gain):
    ms = jnp.mean(x * x, axis=-1, keepdims=True)
    return x * lax.rsqrt(ms + NORM_EPS) * gain


def _sigmoid(x):
    return 1.0 / (1.0 + jnp.exp(-x))


def _params(semantics, vmem_limit_bytes=VMEM_LIMIT_BYTES):
    return pltpu.CompilerParams(dimension_semantics=semantics, vmem_limit_bytes=vmem_limit_bytes)


def _ffn_kernel(x_ref, gpre_ref, gpost_ref, wg_ref, wu_ref, wd_ref, o_ref, xn_ref, *, last_cols):
    j = pl.program_id(1)
    last = pl.num_programs(1) - 1

    @pl.when(j == 0)
    def _():
        xn_ref[...] = _rms(x_ref[...], gpre_ref[...]).astype(BF16)
        o_ref[...] = jnp.zeros_like(o_ref)

    def accumulate(cols):
        xn = xn_ref[...]
        gate = _dot(xn, wg_ref[:, 0:cols])
        up = _dot(xn, wu_ref[:, 0:cols])
        act = (gate * _sigmoid(gate) * up).astype(BF16)
        o_ref[...] += _dot(act, wd_ref[0:cols, :])

    @pl.when(j < last)
    def _():
        accumulate(wg_ref.shape[1])

    @pl.when(j == last)
    def _():
        accumulate(last_cols)
        o_ref[...] = x_ref[...] + MACARON_WEIGHT * _rms(o_ref[...], gpost_ref[...])


def _ffn(x, g_pre, g_post, w_gate, w_up, w_down, *, tm=1024, tf=512):
    t, d = x.shape
    f = w_gate.shape[1]
    vmem = 2 * 2 * tm * d * 4 + tm * d * 2 + 2 * 3 * d * tf * 2 + 3 * tm * tf * 4
    assert vmem <= VMEM_BYTES - (2 << 20)
    steps = pl.cdiv(f, tf)
    kernel = functools.partial(_ffn_kernel, last_cols=f - (steps - 1) * tf)
    return pl.pallas_call(
        kernel,
        grid=(t // tm, steps),
        in_specs=[
            pl.BlockSpec((tm, d), lambda i, j: (i, 0)),
            pl.BlockSpec((1, d), lambda i, j: (0, 0)),
            pl.BlockSpec((1, d), lambda i, j: (0, 0)),
            pl.BlockSpec((d, tf), lambda i, j: (0, j)),
            pl.BlockSpec((d, tf), lambda i, j: (0, j)),
            pl.BlockSpec((tf, d), lambda i, j: (j, 0)),
        ],
        out_specs=pl.BlockSpec((tm, d), lambda i, j: (i, 0)),
        out_shape=jax.ShapeDtypeStruct((t, d), F32),
        scratch_shapes=[pltpu.VMEM((tm, d), BF16)],
        compiler_params=_params(("parallel", "arbitrary"), vmem),
        name="ffn",
    )(x, g_pre, g_post, w_gate, w_up, w_down)


def _token_shift_lerp(z, prev_row, mu):
    rolled = pltpu.roll(z, 1, 0)
    row = lax.broadcasted_iota(jnp.int32, z.shape, 0)
    shifted = jnp.where(row == 0, prev_row, rolled)
    return z + (shifted - z) * mu


def _inproj_kernel(h_ref, g_ref, w_ref, mu_ref, o_ref, u_ref, prev_ref, *, tiles_per_seq, last_cols):
    i = pl.program_id(0)
    j = pl.program_id(1)
    last = pl.num_programs(1) - 1

    @pl.when(j == 0)
    def _():
        u_ref[...] = _rms(h_ref[...], g_ref[...]).astype(BF16)

    def project(cols):
        z = _dot(u_ref[...], w_ref[:, 0:cols])
        first_of_seq = (i % tiles_per_seq) == 0
        prev = jnp.where(first_of_seq, 0.0, prev_ref[j, :, 0:cols])
        prev_ref[j, :, 0:cols] = z[z.shape[0] - 1:z.shape[0], :]
        o_ref[:, 0:cols] = _token_shift_lerp(z, prev, mu_ref[:, 0:cols]).astype(o_ref.dtype)

    @pl.when(j < last)
    def _():
        project(w_ref.shape[1])

    @pl.when(j == last)
    def _():
        project(last_cols)


def _inproj(h, gain, w, mu, seq, *, tm=1024, tn=1024):
    t, d = h.shape
    n = w.shape[1]
    assert seq % tm == 0
    steps = pl.cdiv(n, tn)
    kernel = functools.partial(_inproj_kernel, tiles_per_seq=seq // tm, last_cols=n - (steps - 1) * tn)
    return pl.pallas_call(
        kernel,
        grid=(t // tm, steps),
        in_specs=[
            pl.BlockSpec((tm, d), lambda i, j: (i, 0)),
            pl.BlockSpec((1, d), lambda i, j: (0, 0)),
            pl.BlockSpec((d, tn), lambda i, j: (0, j)),
            pl.BlockSpec((1, tn), lambda i, j: (0, j)),
        ],
        out_specs=pl.BlockSpec((tm, tn), lambda i, j: (i, j)),
        out_shape=jax.ShapeDtypeStruct((t, n), BF16),
        scratch_shapes=[pltpu.VMEM((tm, d), BF16), pltpu.VMEM((steps, 1, tn), F32)],
        compiler_params=_params(("arbitrary", "arbitrary")),
        name="inproj",
    )(h, gain, w, mu)


def _odd_blocks(x, s):
    return jnp.concatenate([x[b * s:(b + 1) * s] for b in range(1, x.shape[0] // s, 2)], axis=0)


def _spread_odd_blocks(u, s):
    zero = jnp.zeros((s, u.shape[1]), u.dtype)
    parts = []
    for b in range(u.shape[0] // s):
        parts += [zero, u[b * s:(b + 1) * s]]
    return jnp.concatenate(parts, axis=0)


def _rwkv_kernel(zr_ref, zl_ref, w0_ref, w2_ref, a0_ref, a2_ref, g2_ref,
                 kk_ref, ka_ref, rk_ref, gnw_ref, gnb_ref, hblk_ref,
                 o_ref,
                 state_ref, nat_s, stk_s, gend_s, bonus_s, gate_s, *, width, ct):
    i = pl.program_id(1)
    n_pairs = width // PAIR
    span = SUBCHUNKS * CHUNK
    n_trips = ct // span
    units = [(sub, p) for sub in range(SUBCHUNKS) for p in range(n_pairs)]
    c2 = 2 * CHUNK
    OP_AT, OP_RT, OP_BT, OP_KT, OP_V, OP_BH, OP_KH = range(7)
    NATURAL = (OP_AT, OP_RT, OP_V, OP_BH, OP_KH)
    STACKED = (OP_AT, OP_BT, OP_KT, OP_V)

    @pl.when(i == 0)
    def _():
        state_ref[...] = jnp.zeros_like(state_ref)

    hblk = hblk_ref[...]

    def head_sum(x):
        xb = x.astype(BF16)
        return jnp.concatenate(
            [_dot(xb[:, q * MXU_DIM:(q + 1) * MXU_DIM], hblk) for q in range(width // MXU_DIM)], axis=1)

    ti = lax.broadcasted_iota(jnp.int32, (CHUNK, PAIR), 0)
    si = lax.broadcasted_iota(jnp.int32, (CHUNK, PAIR), 1) & (CHUNK - 1)
    strict = ti > si
    incl = ti >= si
    eye = (ti == si).astype(F32)

    def level_mask(rows, s):
        cols = lax.broadcasted_iota(jnp.int32, rows.shape, 1) & (CHUNK - 1)
        sh = s.bit_length() - 1
        return ((rows >> (sh + 1)) == (cols >> (sh + 1))) & (((rows >> sh) & 1) == 1) & (((cols >> sh) & 1) == 0)

    levels = []
    s = 1
    while s < CHUNK:
        levels.append((s, level_mask(ti, s) if s < SUBLANES else level_mask(_odd_blocks(ti, s), s)))
        s *= 2
    tri_r = lax.broadcasted_iota(jnp.int32, (span, span), 0)
    tri_c = lax.broadcasted_iota(jnp.int32, (span, span), 1)
    cum_tri = ((tri_r >= tri_c) & ((tri_r >> LOG_CHUNK) == (tri_c >> LOG_CHUNK))).astype(BF16)
    head_masks = {}
    for n in (PAIR, 2 * PAIR):
        first = (lax.broadcasted_iota(jnp.int32, (CHUNK, n), 1) & HEAD_DIM) == 0
        head_masks[n] = (first.astype(BF16), (~first).astype(BF16))
    pr = lax.broadcasted_iota(jnp.int32, (PAIR, 2 * PAIR), 0)
    pc = lax.broadcasted_iota(jnp.int32, (PAIR, 2 * PAIR), 1)
    diag_pp = lax.broadcasted_iota(jnp.int32, (PAIR, PAIR), 0) == lax.broadcasted_iota(jnp.int32, (PAIR, PAIR), 1)
    same_head_pp = (pr >> LOG_CHUNK) == ((pc >> LOG_CHUNK) & 1)
    zeros_st = jnp.zeros((c2, PAIR), BF16)
    zeros_nt = jnp.zeros((CHUNK, PAIR), BF16)

    def stack(xb):
        first, second = head_masks[xb.shape[1]]
        return jnp.concatenate([xb * first, xb * second], axis=0)

    def prepare(rows):
        r = zr_ref[rows, 0:width].astype(F32)
        k = zr_ref[rows, width:2 * width].astype(F32)
        v = zr_ref[rows, 2 * width:3 * width].astype(F32)
        lw = zl_ref[rows, 0:LAT_DECAY].astype(F32)
        la = zl_ref[rows, LAT_DECAY:LAT_DECAY + LAT_AAA]
        lg = zl_ref[rows, LAT_DECAY + LAT_AAA:LAT_COLS].astype(F32)
        yield

        ld = DECAY_SCALE * _sigmoid(w0_ref[...] + _dot(jnp.tanh(lw).astype(BF16), w2_ref[...]))
        a = _sigmoid(a0_ref[...] + _dot(la, a2_ref[...]))
        gate_s[...] = _dot(_sigmoid(lg).astype(BF16), g2_ref[...])
        yield
        kk = k * kk_ref[...]
        kk = kk * lax.rsqrt(jnp.maximum(head_sum(kk * kk), L2_EPS * L2_EPS))
        k2 = k * (1.0 + (a - 1.0) * ka_ref[...])
        bonus_s[...] = head_sum(r * k2 * rk_ref[...]) * v
        bvec = kk * a
        yield

        l1 = ld.astype(BF16)
        l2 = (ld - l1.astype(F32)).astype(BF16)
        gcum = _dot(cum_tri, l1) + _dot(cum_tri, l2)
        gtots = [gcum[(sub + 1) * CHUNK - 1:(sub + 1) * CHUNK, :] for sub in range(SUBCHUNKS)]
        gtot = jnp.concatenate([jnp.broadcast_to(x, (CHUNK, width)) for x in gtots], axis=0)
        e_neg = jnp.exp(-gcum)
        e_end = jnp.exp(gtot - gcum)
        gend_s[...] = jnp.exp(jnp.concatenate(gtots, axis=0))
        full = {OP_AT: -kk * jnp.exp(gcum - ld), OP_RT: r * jnp.exp(gcum), OP_BT: bvec * e_neg, OP_KT: k2 * e_neg,
                OP_V: v, OP_BH: bvec * e_end, OP_KH: k2 * e_end}
        yield
        for op, x in full.items():
            for u, (sub, p) in enumerate(units):
                xb = x[sub * CHUNK:(sub + 1) * CHUNK, p * PAIR:(p + 1) * PAIR].astype(BF16)
                if op in NATURAL:
                    nat_s[NATURAL.index(op), u] = xb
                if op in STACKED:
                    stk_s[STACKED.index(op), u] = stack(xb)
            if op % 2 == 1:
                yield

    def solve(rows):
        pairs = range(len(units))
        at, rt, vn, bh, kh = [[nat_s[o, p] for p in pairs] for o in range(len(NATURAL))]
        at_s, bt_s, kt_s, v_s = [[stk_s[o, p] for p in pairs] for o in range(len(STACKED))]
        g_end = gend_s[...]
        bonus = bonus_s[...]
        gate = gate_s[...]

        scores = [_dot_nt(jnp.concatenate([at[p], rt[p]], axis=0), jnp.concatenate([bt_s[p], kt_s[p]], axis=0)) for p in pairs]
        yield
        a_ab = [jnp.where(strict, sc[0:CHUNK, 0:PAIR], 0.0) for sc in scores]
        a_ak = [jnp.where(strict, sc[0:CHUNK, PAIR:2 * PAIR], 0.0).astype(BF16) for sc in scores]
        a_r = [jnp.concatenate([jnp.where(incl, sc[CHUNK:c2, 0:PAIR], 0.0), jnp.where(incl, sc[CHUNK:c2, PAIR:2 * PAIR], 0.0)],
                               axis=1).astype(BF16) for sc in scores]

        tinv = [eye + jnp.where(levels[0][1], x, 0.0) for x in a_ab]
        yield
        for s, m in levels[1:]:
            tb = [t.astype(BF16) for t in tinv]
            tstk = [stack(t) for t in tb]
            if s < SUBLANES:
                inner = [_dot(jnp.where(m, a_ab[p], 0.0).astype(BF16), tstk[p]).astype(BF16) for p in pairs]
                tinv = [tinv[p] + _dot(tb[p], stack(inner[p])) for p in pairs]
            else:
                inner = [_dot(jnp.where(m, _odd_blocks(a_ab[p], s), 0.0).astype(BF16), tstk[p]) for p in pairs]
                inner = [stack(_spread_odd_blocks(x, s).astype(BF16)) for x in inner]
                upd = [_dot(_odd_blocks(tinv[p], s).astype(BF16), inner[p]) for p in pairs]
                tinv = [tinv[p] + _spread_odd_blocks(upd[p], s) for p in pairs]
            yield
        tb = [t.astype(BF16) for t in tinv]

        akv = [_dot(a_ak[p], v_s[p]).astype(BF16) for p in pairs]
        wub = [_dot(tb[p], jnp.concatenate([at_s[p], stack(akv[p])], axis=1)).astype(BF16) for p in pairs]
        qy = [_dot(a_r[p], jnp.concatenate([stack(wub[p]), jnp.concatenate([zeros_st, v_s[p]], axis=1)], axis=0))
              for p in pairs]
        pz = [_dot_tn(jnp.concatenate([bh[p], kh[p]], axis=0),
                      jnp.concatenate([wub[p], jnp.concatenate([zeros_nt, vn[p]], axis=1)], axis=0)) for p in pairs]
        pz = [jnp.where(same_head_pp, x, 0.0) for x in pz]
        yield
        lhs = []
        for u, (sub, p) in enumerate(units):
            qm = rt[u].astype(F32) + qy[u][:, 0:PAIR]
            p_mat = jnp.where(diag_pp, g_end[sub:sub + 1, p * PAIR:(p + 1) * PAIR], 0.0) + pz[u][:, 0:PAIR]
            lhs.append(jnp.concatenate([qm, p_mat], axis=0).astype(BF16))
        state = [state_ref[p] for p in range(n_pairs)]
        ys = []
        for u, (sub, p) in enumerate(units):
            yv = qy[u][:, PAIR:2 * PAIR]
            both = _dot(lhs[u], state[p].astype(BF16))
            ys.append(both[0:CHUNK, :] + yv)
            state[p] = both[CHUNK:CHUNK + PAIR, :] + pz[u][:, PAIR:2 * PAIR]
        for p in range(n_pairs):
            state_ref[p] = state[p]
        y = jnp.concatenate([jnp.concatenate(ys[sub * n_pairs:(sub + 1) * n_pairs], axis=1) for sub in range(SUBCHUNKS)], axis=0)

        d = y - head_sum(y) * (1.0 / HEAD_DIM)
        var = head_sum(d * d) * (1.0 / HEAD_DIM)
        yn = d * lax.rsqrt(var + GN_EPS) * gnw_ref[...] + gnb_ref[...]
        o_ref[rows, :] = ((yn + bonus) * gate).astype(o_ref.dtype)
        yield

    def chunk_rows(c):
        return pl.ds(pl.multiple_of(c * span, span), span)

    for _ in prepare(pl.ds(0, span)):
        pass

    def chunk_body(c, carry):
        stages = solve(chunk_rows(c))
        next(stages)
        nxt = jnp.minimum(c + 1, n_trips - 1)
        pieces = prepare(chunk_rows(nxt))
        for _ in itertools.zip_longest(pieces, stages):
            pass
        return carry

    lax.fori_loop(0, n_trips, chunk_body, 0)


def _rwkv(p, batch, seq, lat_block, w0, w2p, a0, a2p, g2p, k_k, k_a, r_k, gn_w, gn_b, *, ct=1024):
    width = w0.shape[1]
    n_heads = width // HEAD_DIM
    assert seq % ct == 0 and ct % (SUBCHUNKS * CHUNK) == 0
    steps = seq // ct
    blk = jnp.arange(MXU_DIM, dtype=jnp.int32) // HEAD_DIM
    hblk = (blk[:, None] == blk[None, :]).astype(BF16)
    row = lambda shape: pl.BlockSpec(shape, lambda b, i: (0, 0))
    kernel = functools.partial(_rwkv_kernel, width=width, ct=ct)
    return pl.pallas_call(
        kernel,
        grid=(batch, steps),
        in_specs=[
            pl.BlockSpec((ct, 3 * width), lambda b, i: (b * steps + i, 0)),
            pl.BlockSpec((ct, LAT_COLS), lambda b, i: (b * steps + i, lat_block)),
            row((1, width)), row((LAT_DECAY, width)),
            row((1, width)), row((LAT_AAA, width)),
            row((LAT_GATE, width)),
            row((1, width)), row((1, width)), row((1, width)), row((1, width)), row((1, width)),
            row((MXU_DIM, MXU_DIM)),
        ],
        out_specs=pl.BlockSpec((ct, width), lambda b, i: (b * steps + i, 0)),
        out_shape=jax.ShapeDtypeStruct((batch * seq, width), BF16),
        scratch_shapes=[
            pltpu.VMEM((n_heads // 2, PAIR, PAIR), F32),
            pltpu.VMEM((5, SUBCHUNKS * n_heads // 2, CHUNK, PAIR), BF16),
            pltpu.VMEM((4, SUBCHUNKS * n_heads // 2, 2 * CHUNK, PAIR), BF16),
            pltpu.VMEM((SUBCHUNKS, width), F32), pltpu.VMEM((SUBCHUNKS * CHUNK, width), F32), pltpu.VMEM((SUBCHUNKS * CHUNK, width), F32),
        ],
        compiler_params=_params(("arbitrary", "arbitrary")),
        name="rwkv",
    )(p, p, w0, w2p, a0, a2p, g2p, k_k, k_a, r_k, gn_w, gn_b, hblk)


def _pooled_branch(z, halo, pos, w_ref, scale_ref, group_dim):
    run = jnp.concatenate([halo, z], axis=0)
    span = 1
    outs = []
    for gi, win in enumerate(POOL_WINDOWS):
        cols = slice(gi * group_dim, (gi + 1) * group_dim)
        while span < win:
            run = run + pltpu.roll(run, span, 0)
            span *= 2
        pooled = run[POOL_HALO:, cols] / jnp.minimum(pos, float(win))
        mixed = (pooled - z[:, cols]).astype(BF16)
        outs.append(_dot(mixed, w_ref[gi]) * scale_ref[:, cols])
    return jnp.concatenate(outs, axis=1)


def _mix_kernel(ya_ref, zb_ref, ga_ref, gb_ref, h_ref, pw_ref, ps_ref, wa_ref, wb_ref, wo_ref, gpost_ref, o_ref, halo_ref,
                *, tiles_per_seq):
    i = pl.program_id(0)
    tm = zb_ref.shape[0]
    group_dim = pw_ref.shape[1]
    t0 = (i % tiles_per_seq) * tm

    @pl.when(t0 == 0)
    def _():
        halo_ref[...] = jnp.zeros_like(halo_ref)

    z = zb_ref[...].astype(F32)
    pos = (t0 + lax.broadcasted_iota(jnp.int32, (tm, group_dim), 0) + 1).astype(F32)
    yb = _pooled_branch(z, halo_ref[...], pos, pw_ref, ps_ref, group_dim).astype(BF16)
    halo_ref[...] = z[tm - POOL_HALO:tm, :]

    pa = _dot(ya_ref[...], wa_ref[...])
    pb = _dot(yb, wb_ref[...])
    m = (_sigmoid(ga_ref[...].astype(F32)) * pa + _sigmoid(gb_ref[...].astype(F32)) * pb).astype(BF16)
    mx = _dot(m, wo_ref[...])
    o_ref[...] = h_ref[...] + _rms(mx, gpost_ref[...])


def _mix_out(ya, p, pool_block, ga_block, gb_block, h, pool_w, pool_scale, wa, wb, wo, g_post, seq, *, tm=512):
    t, d = h.shape
    wa_rows, wb_rows = wa.shape[0], wb.shape[0]
    groups, group_dim, _ = pool_w.shape
    assert seq % tm == 0 and groups * group_dim == wb_rows
    const = lambda shape: pl.BlockSpec(shape, lambda i: (0,) * len(shape), pipeline_mode=pl.Buffered(1))
    kernel = functools.partial(_mix_kernel, tiles_per_seq=seq // tm)
    return pl.pallas_call(
        kernel,
        grid=(t // tm,),
        in_specs=[
            pl.BlockSpec((tm, wa_rows), lambda i: (i, 0)),
            pl.BlockSpec((tm, wb_rows), lambda i: (i, pool_block)),
            pl.BlockSpec((tm, d), lambda i: (i, ga_block)),
            pl.BlockSpec((tm, d), lambda i: (i, gb_block)),
            pl.BlockSpec((tm, d), lambda i: (i, 0)),
            const((groups, group_dim, group_dim)), pl.BlockSpec((1, wb_rows), lambda i: (0, 0)),
            const((wa_rows, d)), const((wb_rows, d)), const((d, d)),
            pl.BlockSpec((1, d), lambda i: (0, 0)),
        ],
        out_specs=pl.BlockSpec((tm, d), lambda i: (i, 0)),
        out_shape=jax.ShapeDtypeStruct((t, d), F32),
        scratch_shapes=[pltpu.VMEM((POOL_HALO, wb_rows), F32)],
        compiler_params=_params(("arbitrary",)),
        name="mix_out",
    )(ya, p, p, p, h, pool_w, pool_scale, wa, wb, wo, g_post)


def _pad_cols(w, n):
    return jnp.pad(w, ((0, 0), (0, n - w.shape[1])))


def _pad_rows(w, n):
    return jnp.pad(w, ((0, n - w.shape[0]), (0, 0)))


def kernel(x, ln_ffn1_pre, ln_ffn1_post, ffn1_gate, ffn1_up, ffn1_down, ln_mix_pre, ln_mix_post, w_in, rwkv_mu, rwkv_w0, rwkv_w2, rwkv_a0, rwkv_a2, rwkv_g2, rwkv_k_k, rwkv_k_a, rwkv_r_k, rwkv_gn_w, rwkv_gn_b, w_proj_a, pool_w, pool_scale, w_proj_b, w_out, ln_ffn2_pre, ln_ffn2_post, ffn2_gate, ffn2_up, ffn2_down):
    batch, seq, d = x.shape
    depth = w_in.shape[0]
    width = rwkv_w0.shape[1]
    pool_width = pool_scale.shape[1]
    n_decay, n_aaa, n_gate = rwkv_w2.shape[1], rwkv_a2.shape[1], rwkv_g2.shape[1]
    rkv = 3 * width
    rwkv_cols = rkv + n_decay + n_aaa + n_gate
    assert n_decay <= LAT_DECAY and n_aaa <= LAT_AAA and n_gate <= LAT_GATE
    assert rkv % pool_width == 0 and (rkv + pool_width) % d == 0 and (rkv + pool_width + 2 * d) % LAT_COLS == 0

    h = x.reshape(batch * seq, d)
    row = lambda v: v.reshape(1, -1)
    for l in range(depth):
        h = _ffn(h, row(ln_ffn1_pre[l]), row(ln_ffn1_post[l]),
                 ffn1_gate[l].astype(BF16), ffn1_up[l].astype(BF16), ffn1_down[l].astype(BF16))

        wl = w_in[l].astype(BF16)
        o1, o2 = rkv + n_decay, rkv + n_decay + n_aaa
        w_cat = jnp.concatenate([
            wl[:, :rkv], wl[:, rwkv_cols:],
            _pad_cols(wl[:, rkv:o1], LAT_DECAY), _pad_cols(wl[:, o1:o2], LAT_AAA), _pad_cols(wl[:, o2:rwkv_cols], LAT_GATE),
        ], axis=1)
        mu = rwkv_mu[l]
        mu_cat = jnp.concatenate([
            mu[:rkv], jnp.zeros((w_in.shape[2] - rwkv_cols,), mu.dtype),
            jnp.pad(mu[rkv:o1], (0, LAT_DECAY - n_decay)), jnp.pad(mu[o1:o2], (0, LAT_AAA - n_aaa)),
            jnp.pad(mu[o2:], (0, LAT_GATE - n_gate))])
        p = _inproj(h, row(ln_mix_pre[l]), w_cat, row(mu_cat), seq)

        pool_block = rkv // pool_width
        ga_block = (rkv + pool_width) // d
        lat_block = (rkv + pool_width + 2 * d) // LAT_COLS
        ya = _rwkv(p, batch, seq, lat_block,
                   row(rwkv_w0[l]), _pad_rows(rwkv_w2[l], LAT_DECAY).astype(BF16),
                   row(rwkv_a0[l]), _pad_rows(rwkv_a2[l], LAT_AAA).astype(BF16),
                   _pad_rows(rwkv_g2[l], LAT_GATE).astype(BF16),
                   row(rwkv_k_k[l]), row(rwkv_k_a[l]), row(rwkv_r_k[l]), row(rwkv_gn_w[l]), row(rwkv_gn_b[l]))
        h = _mix_out(ya, p, pool_block, ga_block, ga_block + 1, h, pool_w[l].astype(BF16), row(pool_scale[l]),
                     w_proj_a[l].astype(BF16), w_proj_b[l].astype(BF16), w_out[l].astype(BF16), row(ln_mix_post[l]), seq)

        h = _ffn(h, row(ln_ffn2_pre[l]), row(ln_ffn2_post[l]),
                 ffn2_gate[l].astype(BF16), ffn2_up[l].astype(BF16), ffn2_down[l].astype(BF16))
    return h.reshape(batch, seq, d)
```

```python
import functools
import itertools

import jax
import jax.numpy as jnp
from jax import lax
from jax.experimental import pallas as pl
from jax.experimental.pallas import tpu as pltpu

F32 = jnp.float32
BF16 = jnp.bfloat16

LANES = 128
SUBLANES = 8
MXU_DIM = 256
VMEM_BYTES = 64 * 1024 * 1024
VMEM_LIMIT_BYTES = 56 * 1024 * 1024

NORM_EPS = 1e-6
GN_EPS = 64e-5
L2_EPS = 1e-12
DECAY_SCALE = -0.6065306597126334
MACARON_WEIGHT = 0.5

HEAD_DIM = 64
POOL_WINDOWS = (2, 4, 8, 16)
POOL_HALO = 16
CHUNK = 64
LOG_CHUNK = 6
ROW_SPLIT = 2
SUBCHUNKS = 2
PAIR = 2 * HEAD_DIM
LAT_DECAY, LAT_AAA, LAT_GATE = 128, 128, 256
LAT_COLS = LAT_DECAY + LAT_AAA + LAT_GATE


def _dot(a, b):
    return jnp.dot(a, b, preferred_element_type=F32)


def _dot_nt(a, b):
    return lax.dot_general(a, b, (((1,), (1,)), ((), ())), preferred_element_type=F32)


def _dot_tn(a, b):
    return lax.dot_general(a, b, (((0,), (0,)), ((), ())), preferred_element_type=F32)


def _rms(x, gain):
    ms = jnp.mean(x * x, axis=-1, keepdims=True)
    return x * lax.rsqrt(ms + NORM_EPS) * gain


def _sigmoid(x):
    return 1.0 / (1.0 + jnp.exp(-x))


def _params(semantics, vmem_limit_bytes=VMEM_LIMIT_BYTES):
    return pltpu.CompilerParams(dimension_semantics=semantics, vmem_limit_bytes=vmem_limit_bytes)


def _ffn_kernel(x_ref, gpre_ref, gpost_ref, wg_ref, wu_ref, wd_ref, o_ref, xn_ref, *, last_cols):
    j = pl.program_id(1)
    last = pl.num_programs(1) - 1
    tm = x_ref.shape[0]
    halves = [pl.ds(r * (tm // ROW_SPLIT), tm // ROW_SPLIT) for r in range(ROW_SPLIT)]

    def accumulate(cols, rows, first=False):
        xn = xn_ref[rows, :]
        gate = _dot(xn, wg_ref[:, 0:cols])
        up = _dot(xn, wu_ref[:, 0:cols])
        act = (gate * _sigmoid(gate) * up).astype(BF16)
        down = _dot(act, wd_ref[0:cols, :])
        o_ref[rows, :] = down if first else o_ref[rows, :] + down

    @pl.when(j == 0)
    def _():
        for rows in halves:
            xn_ref[rows, :] = _rms(x_ref[rows, :], gpre_ref[...]).astype(BF16)
            accumulate(wg_ref.shape[1], rows, first=True)

    @pl.when((j > 0) & (j < last))
    def _():
        accumulate(wg_ref.shape[1], pl.ds(0, tm))

    @pl.when(j == last)
    def _():
        for rows in halves:
            accumulate(last_cols, rows)
            o_ref[rows, :] = x_ref[rows, :] + MACARON_WEIGHT * _rms(o_ref[rows, :], gpost_ref[...])


def _ffn(x, g_pre, g_post, w_gate, w_up, w_down, *, tm=1024, tf=512):
    t, d = x.shape
    f = w_gate.shape[1]
    vmem = 2 * 2 * tm * d * 4 + tm * d * 2 + 2 * 3 * d * tf * 2 + 3 * tm * tf * 4
    assert vmem <= VMEM_BYTES - (2 << 20)
    steps = pl.cdiv(f, tf)
    kernel = functools.partial(_ffn_kernel, last_cols=f - (steps - 1) * tf)
    return pl.pallas_call(
        kernel,
        grid=(t // tm, steps),
        in_specs=[
            pl.BlockSpec((tm, d), lambda i, j: (i, 0)),
            pl.BlockSpec((1, d), lambda i, j: (0, 0)),
            pl.BlockSpec((1, d), lambda i, j: (0, 0)),
            pl.BlockSpec((d, tf), lambda i, j: (0, j)),
            pl.BlockSpec((d, tf), lambda i, j: (0, j)),
            pl.BlockSpec((tf, d), lambda i, j: (j, 0)),
        ],
        out_specs=pl.BlockSpec((tm, d), lambda i, j: (i, 0)),
        out_shape=jax.ShapeDtypeStruct((t, d), F32),
        scratch_shapes=[pltpu.VMEM((tm, d), BF16)],
        compiler_params=_params(("parallel", "arbitrary"), vmem),
        name="ffn",
    )(x, g_pre, g_post, w_gate, w_up, w_down)


def _token_shift_lerp(z, prev_row, mu):
    rolled = pltpu.roll(z, 1, 0)
    row = lax.broadcasted_iota(jnp.int32, z.shape, 0)
    shifted = jnp.where(row == 0, prev_row, rolled)
    return z + (shifted - z) * mu


def _inproj_kernel(h_ref, g_ref, w_ref, mu_ref, o_ref, u_ref, prev_ref, *, tiles_per_seq, first_cols):
    i = pl.program_id(0)
    j = pl.program_id(1)
    tm = h_ref.shape[0]

    def finish(z, cols):
        first_of_seq = (i % tiles_per_seq) == 0
        prev = jnp.where(first_of_seq, 0.0, prev_ref[j, :, 0:cols])
        prev_ref[j, :, 0:cols] = z[tm - 1:tm, :]
        o_ref[:, 0:cols] = _token_shift_lerp(z, prev, mu_ref[:, 0:cols]).astype(o_ref.dtype)

    @pl.when(j == 0)
    def _():
        parts = []
        for r in range(ROW_SPLIT):
            rows = pl.ds(r * (tm // ROW_SPLIT), tm // ROW_SPLIT)
            u_ref[rows, :] = _rms(h_ref[rows, :], g_ref[...]).astype(BF16)
            parts.append(_dot(u_ref[rows, :], w_ref[:, 0:first_cols]))
        finish(jnp.concatenate(parts, axis=0), first_cols)

    @pl.when(j > 0)
    def _():
        finish(_dot(u_ref[...], w_ref[...]), w_ref.shape[1])


def _inproj(h, gain, w, mu, seq, *, tm=1024, tn=1024):
    t, d = h.shape
    n = w.shape[1]
    assert seq % tm == 0
    steps = pl.cdiv(n, tn)
    kernel = functools.partial(_inproj_kernel, tiles_per_seq=seq // tm, first_cols=n - (steps - 1) * tn)
    block = lambda j: (j + steps - 1) % steps
    return pl.pallas_call(
        kernel,
        grid=(t // tm, steps),
        in_specs=[
            pl.BlockSpec((tm, d), lambda i, j: (i, 0)),
            pl.BlockSpec((1, d), lambda i, j: (0, 0)),
            pl.BlockSpec((d, tn), lambda i, j: (0, block(j))),
            pl.BlockSpec((1, tn), lambda i, j: (0, block(j))),
        ],
        out_specs=pl.BlockSpec((tm, tn), lambda i, j: (i, block(j))),
        out_shape=jax.ShapeDtypeStruct((t, n), BF16),
        scratch_shapes=[pltpu.VMEM((tm, d), BF16), pltpu.VMEM((steps, 1, tn), F32)],
        compiler_params=_params(("arbitrary", "arbitrary")),
        name="inproj",
    )(h, gain, w, mu)


def _odd_blocks(x, s):
    return jnp.concatenate([x[b * s:(b + 1) * s] for b in range(1, x.shape[0] // s, 2)], axis=0)


def _spread_odd_blocks(u, s):
    zero = jnp.zeros((s, u.shape[1]), u.dtype)
    parts = []
    for b in range(u.shape[0] // s):
        parts += [zero, u[b * s:(b + 1) * s]]
    return jnp.concatenate(parts, axis=0)


def _rwkv_kernel(zr_ref, zl_ref, w0_ref, w2_ref, a0_ref, a2_ref, g2_ref,
                 kk_ref, ka_ref, rk_ref, gnw_ref, gnb_ref, hblk_ref,
                 o_ref,
                 state_ref, nat_s, stk_s, gend_s, bonus_s, gate_s, *, width, ct):
    i = pl.program_id(1)
    n_pairs = width // PAIR
    span = SUBCHUNKS * CHUNK
    n_trips = ct // span
    units = [(sub, p) for sub in range(SUBCHUNKS) for p in range(n_pairs)]
    c2 = 2 * CHUNK
    OP_AT, OP_RT, OP_BT, OP_KT, OP_V, OP_BH, OP_KH = range(7)
    NATURAL = (OP_AT, OP_RT, OP_V, OP_BH, OP_KH)
    STACKED = (OP_AT, OP_BT, OP_KT, OP_V)

    @pl.when(i == 0)
    def _():
        state_ref[...] = jnp.zeros_like(state_ref)

    hblk = hblk_ref[...]

    def head_sum(x):
        xb = x.astype(BF16)
        return jnp.concatenate(
            [_dot(xb[:, q * MXU_DIM:(q + 1) * MXU_DIM], hblk) for q in range(width // MXU_DIM)], axis=1)

    ti = lax.broadcasted_iota(jnp.int32, (CHUNK, PAIR), 0)
    si = lax.broadcasted_iota(jnp.int32, (CHUNK, PAIR), 1) & (CHUNK - 1)
    strict = ti > si
    incl = ti >= si
    eye = (ti == si).astype(F32)

    def level_mask(rows, s):
        cols = lax.broadcasted_iota(jnp.int32, rows.shape, 1) & (CHUNK - 1)
        sh = s.bit_length() - 1
        return ((rows >> (sh + 1)) == (cols >> (sh + 1))) & (((rows >> sh) & 1) == 1) & (((cols >> sh) & 1) == 0)

    levels = []
    s = 1
    while s < CHUNK:
        levels.append((s, level_mask(ti, s) if s < SUBLANES else level_mask(_odd_blocks(ti, s), s)))
        s *= 2
    tri_r = lax.broadcasted_iota(jnp.int32, (span, span), 0)
    tri_c = lax.broadcasted_iota(jnp.int32, (span, span), 1)
    cum_tri = ((tri_r >= tri_c) & ((tri_r >> LOG_CHUNK) == (tri_c >> LOG_CHUNK))).astype(BF16)
    head_masks = {}
    for n in (PAIR, 2 * PAIR):
        first = (lax.broadcasted_iota(jnp.int32, (CHUNK, n), 1) & HEAD_DIM) == 0
        head_masks[n] = (first.astype(BF16), (~first).astype(BF16))
    pr = lax.broadcasted_iota(jnp.int32, (PAIR, 2 * PAIR), 0)
    pc = lax.broadcasted_iota(jnp.int32, (PAIR, 2 * PAIR), 1)
    diag_pp = lax.broadcasted_iota(jnp.int32, (PAIR, PAIR), 0) == lax.broadcasted_iota(jnp.int32, (PAIR, PAIR), 1)
    same_head_pp = (pr >> LOG_CHUNK) == ((pc >> LOG_CHUNK) & 1)
    zeros_st = jnp.zeros((c2, PAIR), BF16)
    zeros_nt = jnp.zeros((CHUNK, PAIR), BF16)

    def stack(xb):
        first, second = head_masks[xb.shape[1]]
        return jnp.concatenate([xb * first, xb * second], axis=0)

    def prepare(rows):
        r = zr_ref[rows, 0:width].astype(F32)
        k = zr_ref[rows, width:2 * width].astype(F32)
        v = zr_ref[rows, 2 * width:3 * width].astype(F32)
        lw = zl_ref[rows, 0:LAT_DECAY].astype(F32)
        la = zl_ref[rows, LAT_DECAY:LAT_DECAY + LAT_AAA]
        lg = zl_ref[rows, LAT_DECAY + LAT_AAA:LAT_COLS].astype(F32)
        yield

        ld = DECAY_SCALE * _sigmoid(w0_ref[...] + _dot(jnp.tanh(lw).astype(BF16), w2_ref[...]))
        a = _sigmoid(a0_ref[...] + _dot(la, a2_ref[...]))
        gate_s[...] = _dot(_sigmoid(lg).astype(BF16), g2_ref[...])
        yield
        kk = k * kk_ref[...]
        kk = kk * lax.rsqrt(jnp.maximum(head_sum(kk * kk), L2_EPS * L2_EPS))
        k2 = k * (1.0 + (a - 1.0) * ka_ref[...])
        bonus_s[...] = head_sum(r * k2 * rk_ref[...]) * v
        bvec = kk * a
        yield

        l1 = ld.astype(BF16)
        l2 = (ld - l1.astype(F32)).astype(BF16)
        gcum = _dot(cum_tri, l1) + _dot(cum_tri, l2)
        gtots = [gcum[(sub + 1) * CHUNK - 1:(sub + 1) * CHUNK, :] for sub in range(SUBCHUNKS)]
        gtot = jnp.concatenate([jnp.broadcast_to(x, (CHUNK, width)) for x in gtots], axis=0)
        e_neg = jnp.exp(-gcum)
        e_end = jnp.exp(gtot - gcum)
        gend_s[...] = jnp.exp(jnp.concatenate(gtots, axis=0))
        full = {OP_AT: -kk * jnp.exp(gcum - ld), OP_RT: r * jnp.exp(gcum), OP_BT: bvec * e_neg, OP_KT: k2 * e_neg,
                OP_V: v, OP_BH: bvec * e_end, OP_KH: k2 * e_end}
        yield
        for op, x in full.items():
            for u, (sub, p) in enumerate(units):
                xb = x[sub * CHUNK:(sub + 1) * CHUNK, p * PAIR:(p + 1) * PAIR].astype(BF16)
                if op in NATURAL:
                    nat_s[NATURAL.index(op), u] = xb
                if op in STACKED:
                    stk_s[STACKED.index(op), u] = stack(xb)
            if op % 2 == 1:
                yield

    def solve(rows):
        pairs = range(len(units))
        at, rt, vn, bh, kh = [[nat_s[o, p] for p in pairs] for o in range(len(NATURAL))]
        at_s, bt_s, kt_s, v_s = [[stk_s[o, p] for p in pairs] for o in range(len(STACKED))]
        g_end = gend_s[...]
        bonus = bonus_s[...]
        gate = gate_s[...]

        scores = [_dot_nt(jnp.concatenate([at[p], rt[p]], axis=0), jnp.concatenate([bt_s[p], kt_s[p]], axis=0)) for p in pairs]
        yield
        a_ab = [jnp.where(strict, sc[0:CHUNK, 0:PAIR], 0.0) for sc in scores]
        a_ak = [jnp.where(strict, sc[0:CHUNK, PAIR:2 * PAIR], 0.0).astype(BF16) for sc in scores]
        a_r = [jnp.concatenate([jnp.where(incl, sc[CHUNK:c2, 0:PAIR], 0.0), jnp.where(incl, sc[CHUNK:c2, PAIR:2 * PAIR], 0.0)],
                               axis=1).astype(BF16) for sc in scores]

        tinv = [eye + jnp.where(levels[0][1], x, 0.0) for x in a_ab]
        yield
        for s, m in levels[1:]:
            tb = [t.astype(BF16) for t in tinv]
            tstk = [stack(t) for t in tb]
            if s < SUBLANES:
                inner = [_dot(jnp.where(m, a_ab[p], 0.0).astype(BF16), tstk[p]).astype(BF16) for p in pairs]
                tinv = [tinv[p] + _dot(tb[p], stack(inner[p])) for p in pairs]
            else:
                inner = [_dot(jnp.where(m, _odd_blocks(a_ab[p], s), 0.0).astype(BF16), tstk[p]) for p in pairs]
                inner = [stack(_spread_odd_blocks(x, s).astype(BF16)) for x in inner]
                upd = [_dot(_odd_blocks(tinv[p], s).astype(BF16), inner[p]) for p in pairs]
                tinv = [tinv[p] + _spread_odd_blocks(upd[p], s) for p in pairs]
            yield
        tb = [t.astype(BF16) for t in tinv]

        akv = [_dot(a_ak[p], v_s[p]).astype(BF16) for p in pairs]
        wub = [_dot(tb[p], jnp.concatenate([at_s[p], stack(akv[p])], axis=1)).astype(BF16) for p in pairs]
        qy = [_dot(a_r[p], jnp.concatenate([stack(wub[p]), jnp.concatenate([zeros_st, v_s[p]], axis=1)], axis=0))
              for p in pairs]
        pz = [_dot_tn(jnp.concatenate([bh[p], kh[p]], axis=0),
                      jnp.concatenate([wub[p], jnp.concatenate([zeros_nt, vn[p]], axis=1)], axis=0)) for p in pairs]
        pz = [jnp.where(same_head_pp, x, 0.0) for x in pz]
        yield
        lhs = []
        for u, (sub, p) in enumerate(units):
            qm = rt[u].astype(F32) + qy[u][:, 0:PAIR]
            p_mat = jnp.where(diag_pp, g_end[sub:sub + 1, p * PAIR:(p + 1) * PAIR], 0.0) + pz[u][:, 0:PAIR]
            lhs.append(jnp.concatenate([qm, p_mat], axis=0).astype(BF16))
        state = [state_ref[p] for p in range(n_pairs)]
        ys = []
        for u, (sub, p) in enumerate(units):
            yv = qy[u][:, PAIR:2 * PAIR]
            both = _dot(lhs[u], state[p].astype(BF16))
            ys.append(both[0:CHUNK, :] + yv)
            state[p] = both[CHUNK:CHUNK + PAIR, :] + pz[u][:, PAIR:2 * PAIR]
        for p in range(n_pairs):
            state_ref[p] = state[p]
        y = jnp.concatenate([jnp.concatenate(ys[sub * n_pairs:(sub + 1) * n_pairs], axis=1) for sub in range(SUBCHUNKS)], axis=0)

        d = y - head_sum(y) * (1.0 / HEAD_DIM)
        var = head_sum(d * d) * (1.0 / HEAD_DIM)
        yn = d * lax.rsqrt(var + GN_EPS) * gnw_ref[...] + gnb_ref[...]
        o_ref[rows, :] = ((yn + bonus) * gate).astype(o_ref.dtype)
        yield

    def chunk_rows(c):
        return pl.ds(pl.multiple_of(c * span, span), span)

    for _ in prepare(pl.ds(0, span)):
        pass

    def chunk_body(c, carry):
        stages = solve(chunk_rows(c))
        next(stages)
        nxt = jnp.minimum(c + 1, n_trips - 1)
        pieces = prepare(chunk_rows(nxt))
        for _ in itertools.zip_longest(pieces, stages):
            pass
        return carry

    lax.fori_loop(0, n_trips, chunk_body, 0)


def _rwkv(p, batch, seq, lat_block, w0, w2p, a0, a2p, g2p, k_k, k_a, r_k, gn_w, gn_b, *, ct=2048):
    width = w0.shape[1]
    n_heads = width // HEAD_DIM
    assert seq % ct == 0 and ct % (SUBCHUNKS * CHUNK) == 0
    steps = seq // ct
    blk = jnp.arange(MXU_DIM, dtype=jnp.int32) // HEAD_DIM
    hblk = (blk[:, None] == blk[None, :]).astype(BF16)
    row = lambda shape: pl.BlockSpec(shape, lambda b, i: (0, 0))
    kernel = functools.partial(_rwkv_kernel, width=width, ct=ct)
    return pl.pallas_call(
        kernel,
        grid=(batch, steps),
        in_specs=[
            pl.BlockSpec((ct, 3 * width), lambda b, i: (b * steps + i, 0)),
            pl.BlockSpec((ct, LAT_COLS), lambda b, i: (b * steps + i, lat_block)),
            row((1, width)), row((LAT_DECAY, width)),
            row((1, width)), row((LAT_AAA, width)),
            row((LAT_GATE, width)),
            row((1, width)), row((1, width)), row((1, width)), row((1, width)), row((1, width)),
            row((MXU_DIM, MXU_DIM)),
        ],
        out_specs=pl.BlockSpec((ct, width), lambda b, i: (b * steps + i, 0)),
        out_shape=jax.ShapeDtypeStruct((batch * seq, width), BF16),
        scratch_shapes=[
            pltpu.VMEM((n_heads // 2, PAIR, PAIR), F32),
            pltpu.VMEM((5, SUBCHUNKS * n_heads // 2, CHUNK, PAIR), BF16),
            pltpu.VMEM((4, SUBCHUNKS * n_heads // 2, 2 * CHUNK, PAIR), BF16),
            pltpu.VMEM((SUBCHUNKS, width), F32), pltpu.VMEM((SUBCHUNKS * CHUNK, width), F32), pltpu.VMEM((SUBCHUNKS * CHUNK, width), F32),
        ],
        compiler_params=_params(("arbitrary", "arbitrary")),
        name="rwkv",
    )(p, p, w0, w2p, a0, a2p, g2p, k_k, k_a, r_k, gn_w, gn_b, hblk)


def _pooled_branch(z, halo, pos, w_ref, scale_ref, group_dim):
    run = jnp.concatenate([halo, z], axis=0)
    span = 1
    outs = []
    for gi, win in enumerate(POOL_WINDOWS):
        cols = slice(gi * group_dim, (gi + 1) * group_dim)
        while span < win:
            run = run + pltpu.roll(run, span, 0)
            span *= 2
        pooled = run[POOL_HALO:, cols] / jnp.minimum(pos, float(win))
        mixed = (pooled - z[:, cols]).astype(BF16)
        outs.append(_dot(mixed, w_ref[gi]) * scale_ref[:, cols])
    return jnp.concatenate(outs, axis=1)


def _mix_kernel(ya_ref, zb_ref, ga_ref, gb_ref, h_ref, pw_ref, ps_ref, wa_ref, wb_ref, wo_ref, gpost_ref, o_ref, halo_ref,
                *, tiles_per_seq):
    i = pl.program_id(0)
    tm = zb_ref.shape[0]
    group_dim = pw_ref.shape[1]
    t0 = (i % tiles_per_seq) * tm

    @pl.when(t0 == 0)
    def _():
        halo_ref[...] = jnp.zeros_like(halo_ref)

    z = zb_ref[...].astype(F32)
    pos = (t0 + lax.broadcasted_iota(jnp.int32, (tm, group_dim), 0) + 1).astype(F32)
    yb = _pooled_branch(z, halo_ref[...], pos, pw_ref, ps_ref, group_dim).astype(BF16)
    halo_ref[...] = z[tm - POOL_HALO:tm, :]

    pa = _dot(ya_ref[...], wa_ref[...])
    pb = _dot(yb, wb_ref[...])
    m = (_sigmoid(ga_ref[...].astype(F32)) * pa + _sigmoid(gb_ref[...].astype(F32)) * pb).astype(BF16)
    mx = _dot(m, wo_ref[...])
    o_ref[...] = h_ref[...] + _rms(mx, gpost_ref[...])


def _mix_out(ya, p, pool_block, ga_block, gb_block, h, pool_w, pool_scale, wa, wb, wo, g_post, seq, *, tm=512):
    t, d = h.shape
    wa_rows, wb_rows = wa.shape[0], wb.shape[0]
    groups, group_dim, _ = pool_w.shape
    assert seq % tm == 0 and groups * group_dim == wb_rows
    const = lambda shape: pl.BlockSpec(shape, lambda i: (0,) * len(shape), pipeline_mode=pl.Buffered(1))
    kernel = functools.partial(_mix_kernel, tiles_per_seq=seq // tm)
    return pl.pallas_call(
        kernel,
        grid=(t // tm,),
        in_specs=[
            pl.BlockSpec((tm, wa_rows), lambda i: (i, 0)),
            pl.BlockSpec((tm, wb_rows), lambda i: (i, pool_block)),
            pl.BlockSpec((tm, d), lambda i: (i, ga_block)),
            pl.BlockSpec((tm, d), lambda i: (i, gb_block)),
            pl.BlockSpec((tm, d), lambda i: (i, 0)),
            const((groups, group_dim, group_dim)), pl.BlockSpec((1, wb_rows), lambda i: (0, 0)),
            const((wa_rows, d)), const((wb_rows, d)), const((d, d)),
            pl.BlockSpec((1, d), lambda i: (0, 0)),
        ],
        out_specs=pl.BlockSpec((tm, d), lambda i: (i, 0)),
        out_shape=jax.ShapeDtypeStruct((t, d), F32),
        scratch_shapes=[pltpu.VMEM((POOL_HALO, wb_rows), F32)],
        compiler_params=_params(("arbitrary",)),
        name="mix_out",
    )(ya, p, p, p, h, pool_w, pool_scale, wa, wb, wo, g_post)


def _pad_cols(w, n):
    return jnp.pad(w, ((0, 0), (0, n - w.shape[1])))


def _pad_rows(w, n):
    return jnp.pad(w, ((0, n - w.shape[0]), (0, 0)))


def kernel(x, ln_ffn1_pre, ln_ffn1_post, ffn1_gate, ffn1_up, ffn1_down, ln_mix_pre, ln_mix_post, w_in, rwkv_mu, rwkv_w0, rwkv_w2, rwkv_a0, rwkv_a2, rwkv_g2, rwkv_k_k, rwkv_k_a, rwkv_r_k, rwkv_gn_w, rwkv_gn_b, w_proj_a, pool_w, pool_scale, w_proj_b, w_out, ln_ffn2_pre, ln_ffn2_post, ffn2_gate, ffn2_up, ffn2_down):
    batch, seq, d = x.shape
    depth = w_in.shape[0]
    width = rwkv_w0.shape[1]
    pool_width = pool_scale.shape[1]
    n_decay, n_aaa, n_gate = rwkv_w2.shape[1], rwkv_a2.shape[1], rwkv_g2.shape[1]
    rkv = 3 * width
    rwkv_cols = rkv + n_decay + n_aaa + n_gate
    assert n_decay <= LAT_DECAY and n_aaa <= LAT_AAA and n_gate <= LAT_GATE
    assert rkv % pool_width == 0 and (rkv + pool_width) % d == 0 and (rkv + pool_width + 2 * d) % LAT_COLS == 0

    h = x.reshape(batch * seq, d)
    row = lambda v: v.reshape(1, -1)
    for l in range(depth):
        h = _ffn(h, row(ln_ffn1_pre[l]), row(ln_ffn1_post[l]),
                 ffn1_gate[l].astype(BF16), ffn1_up[l].astype(BF16), ffn1_down[l].astype(BF16))

        wl = w_in[l].astype(BF16)
        o1, o2 = rkv + n_decay, rkv + n_decay + n_aaa
        w_cat = jnp.concatenate([
            wl[:, :rkv], wl[:, rwkv_cols:],
            _pad_cols(wl[:, rkv:o1], LAT_DECAY), _pad_cols(wl[:, o1:o2], LAT_AAA), _pad_cols(wl[:, o2:rwkv_cols], LAT_GATE),
        ], axis=1)
        mu = rwkv_mu[l]
        mu_cat = jnp.concatenate([
            mu[:rkv], jnp.zeros((w_in.shape[2] - rwkv_cols,), mu.dtype),
            jnp.pad(mu[rkv:o1], (0, LAT_DECAY - n_decay)), jnp.pad(mu[o1:o2], (0, LAT_AAA - n_aaa)),
            jnp.pad(mu[o2:], (0, LAT_GATE - n_gate))])
        p = _inproj(h, row(ln_mix_pre[l]), w_cat, row(mu_cat), seq)

        pool_block = rkv // pool_width
        ga_block = (rkv + pool_width) // d
        lat_block = (rkv + pool_width + 2 * d) // LAT_COLS
        ya = _rwkv(p, batch, seq, lat_block,
                   row(rwkv_w0[l]), _pad_rows(rwkv_w2[l], LAT_DECAY).astype(BF16),
                   row(rwkv_a0[l]), _pad_rows(rwkv_a2[l], LAT_AAA).astype(BF16),
                   _pad_rows(rwkv_g2[l], LAT_GATE).astype(BF16),
                   row(rwkv_k_k[l]), row(rwkv_k_a[l]), row(rwkv_r_k[l]), row(rwkv_gn_w[l]), row(rwkv_gn_b[l]))
        h = _mix_out(ya, p, pool_block, ga_block, ga_block + 1, h, pool_w[l].astype(BF16), row(pool_scale[l]),
                     w_proj_a[l].astype(BF16), w_proj_b[l].astype(BF16), w_out[l].astype(BF16), row(ln_mix_post[l]), seq)

        h = _ffn(h, row(ln_ffn2_pre[l]), row(ln_ffn2_post[l]),
                 ffn2_gate[l].astype(BF16), ffn2_up[l].astype(BF16), ffn2_down[l].astype(BF16))
    return h.reshape(batch, seq, d)
```

```python
import functools
import itertools
import math

import jax
import jax.numpy as jnp
from jax import lax
from jax.experimental import pallas as pl
from jax.experimental.pallas import tpu as pltpu

F32 = jnp.float32
BF16 = jnp.bfloat16

SUBLANES = 8
MXU_DIM = 256
VMEM_BYTES = 64 * 1024 * 1024
VMEM_LIMIT_BYTES = 56 * 1024 * 1024

NORM_EPS = 1e-6
GN_EPS = 64e-5
L2_EPS = 1e-12
DECAY_SCALE = -math.exp(-0.5)
MACARON_WEIGHT = 0.5

HEAD_DIM = 64
POOL_WINDOWS = (2, 4, 8, 16)
POOL_HALO = max(POOL_WINDOWS)
CHUNK = 64
LOG_CHUNK = CHUNK.bit_length() - 1
ROW_SPLIT = 2
SUBCHUNKS = 2
PAIR = 2 * HEAD_DIM
LAT_DECAY, LAT_AAA, LAT_GATE = 128, 128, 256
LAT_COLS = LAT_DECAY + LAT_AAA + LAT_GATE


def _dot(a, b):
    return jnp.dot(a, b, preferred_element_type=F32)


def _dot_nt(a, b):
    return lax.dot_general(a, b, (((1,), (1,)), ((), ())), preferred_element_type=F32)


def _dot_tn(a, b):
    return lax.dot_general(a, b, (((0,), (0,)), ((), ())), preferred_element_type=F32)


def _rms(x, gain):
    ms = jnp.mean(x * x, axis=-1, keepdims=True)
    return x * lax.rsqrt(ms + NORM_EPS) * gain


def _sigmoid(x):
    return 1.0 / (1.0 + jnp.exp(-x))


def _sigmoid_t(x):
    return 0.5 * jnp.tanh(0.5 * x) + 0.5


def _params(semantics, vmem_limit_bytes=VMEM_LIMIT_BYTES):
    return pltpu.CompilerParams(dimension_semantics=semantics, vmem_limit_bytes=vmem_limit_bytes)


def _ffn_kernel(x_ref, gpre_ref, gpost_ref, wg_ref, wu_ref, wd_ref, o_ref, xn_ref, *, last_cols):
    j = pl.program_id(1)
    last = pl.num_programs(1) - 1
    tm = x_ref.shape[0]
    halves = [pl.ds(r * (tm // ROW_SPLIT), tm // ROW_SPLIT) for r in range(ROW_SPLIT)]

    def accumulate(cols, rows, first=False):
        xn = xn_ref[rows, :]
        gate = _dot(xn, wg_ref[:, 0:cols])
        up = _dot(xn, wu_ref[:, 0:cols])
        act = (gate * _sigmoid(gate) * up).astype(BF16)
        down = _dot(act, wd_ref[0:cols, :])
        o_ref[rows, :] = down if first else o_ref[rows, :] + down

    @pl.when(j == 0)
    def _():
        for rows in halves:
            xn_ref[rows, :] = _rms(x_ref[rows, :], gpre_ref[...]).astype(BF16)
            accumulate(wg_ref.shape[1], rows, first=True)

    @pl.when((j > 0) & (j < last))
    def _():
        accumulate(wg_ref.shape[1], pl.ds(0, tm))

    @pl.when(j == last)
    def _():
        for rows in halves:
            accumulate(last_cols, rows)
            o_ref[rows, :] = x_ref[rows, :] + MACARON_WEIGHT * _rms(o_ref[rows, :], gpost_ref[...])


def _ffn(x, g_pre, g_post, w_gate, w_up, w_down, *, tm=1024, tf=512):
    t, d = x.shape
    f = w_gate.shape[1]
    vmem = 2 * 2 * tm * d * 4 + tm * d * 2 + 2 * 3 * d * tf * 2 + 3 * tm * tf * 4
    assert vmem <= VMEM_BYTES - (2 << 20)
    steps = pl.cdiv(f, tf)
    kernel = functools.partial(_ffn_kernel, last_cols=f - (steps - 1) * tf)
    return pl.pallas_call(
        kernel,
        grid=(t // tm, steps),
        in_specs=[
            pl.BlockSpec((tm, d), lambda i, j: (i, 0)),
            pl.BlockSpec((1, d), lambda i, j: (0, 0)),
            pl.BlockSpec((1, d), lambda i, j: (0, 0)),
            pl.BlockSpec((d, tf), lambda i, j: (0, j)),
            pl.BlockSpec((d, tf), lambda i, j: (0, j)),
            pl.BlockSpec((tf, d), lambda i, j: (j, 0)),
        ],
        out_specs=pl.BlockSpec((tm, d), lambda i, j: (i, 0)),
        out_shape=jax.ShapeDtypeStruct((t, d), F32),
        scratch_shapes=[pltpu.VMEM((tm, d), BF16)],
        compiler_params=_params(("parallel", "arbitrary"), vmem),
        name="ffn",
    )(x, g_pre, g_post, w_gate, w_up, w_down)


def _token_shift_lerp(z, prev_row, mu):
    rolled = pltpu.roll(z, 1, 0)
    row = lax.broadcasted_iota(jnp.int32, z.shape, 0)
    shifted = jnp.where(row == 0, prev_row, rolled)
    return z + (shifted - z) * mu


def _inproj_kernel(h_ref, g_ref, w_ref, mu_ref, o_ref, u_ref, prev_ref, *, tiles_per_seq, first_cols):
    i = pl.program_id(0)
    j = pl.program_id(1)
    tm = h_ref.shape[0]

    def finish(z, cols):
        first_of_seq = (i % tiles_per_seq) == 0
        prev = jnp.where(first_of_seq, 0.0, prev_ref[j, :, 0:cols])
        prev_ref[j, :, 0:cols] = z[tm - 1:tm, :]
        o_ref[:, 0:cols] = _token_shift_lerp(z, prev, mu_ref[:, 0:cols]).astype(o_ref.dtype)

    @pl.when(j == 0)
    def _():
        parts = []
        for r in range(ROW_SPLIT):
            rows = pl.ds(r * (tm // ROW_SPLIT), tm // ROW_SPLIT)
            u_ref[rows, :] = _rms(h_ref[rows, :], g_ref[...]).astype(BF16)
            parts.append(_dot(u_ref[rows, :], w_ref[:, 0:first_cols]))
        finish(jnp.concatenate(parts, axis=0), first_cols)

    @pl.when(j > 0)
    def _():
        finish(_dot(u_ref[...], w_ref[...]), w_ref.shape[1])


def _inproj(h, gain, w, mu, seq, *, tm=1024, tn=1024):
    t, d = h.shape
    n = w.shape[1]
    assert seq % tm == 0
    steps = pl.cdiv(n, tn)
    kernel = functools.partial(_inproj_kernel, tiles_per_seq=seq // tm, first_cols=n - (steps - 1) * tn)
    block = lambda j: (j + steps - 1) % steps
    return pl.pallas_call(
        kernel,
        grid=(t // tm, steps),
        in_specs=[
            pl.BlockSpec((tm, d), lambda i, j: (i, 0)),
            pl.BlockSpec((1, d), lambda i, j: (0, 0)),
            pl.BlockSpec((d, tn), lambda i, j: (0, block(j))),
            pl.BlockSpec((1, tn), lambda i, j: (0, block(j))),
        ],
        out_specs=pl.BlockSpec((tm, tn), lambda i, j: (i, block(j))),
        out_shape=jax.ShapeDtypeStruct((t, n), BF16),
        scratch_shapes=[pltpu.VMEM((tm, d), BF16), pltpu.VMEM((steps, 1, tn), F32)],
        compiler_params=_params(("arbitrary", "arbitrary")),
        name="inproj",
    )(h, gain, w, mu)


def _odd_blocks(x, s):
    return jnp.concatenate([x[b * s:(b + 1) * s] for b in range(1, x.shape[0] // s, 2)], axis=0)


def _spread_odd_blocks(u, s):
    zero = jnp.zeros((s, u.shape[1]), u.dtype)
    parts = []
    for b in range(u.shape[0] // s):
        parts += [zero, u[b * s:(b + 1) * s]]
    return jnp.concatenate(parts, axis=0)


def _rwkv_kernel(zr_ref, zl_ref, w0_ref, w2_ref, a0_ref, a2_ref, g2_ref,
                 kk_ref, ka_ref, rk_ref, gnw_ref, gnb_ref, hblk_ref,
                 o_ref,
                 state_ref, nat_s, stk_s, gend_s, bonus_s, gate_s, *, width, ct):
    i = pl.program_id(1)
    n_pairs = width // PAIR
    span = SUBCHUNKS * CHUNK
    n_trips = ct // span
    units = [(sub, p) for sub in range(SUBCHUNKS) for p in range(n_pairs)]
    c2 = 2 * CHUNK
    OP_AT, OP_RT, OP_BT, OP_KT, OP_V, OP_BH, OP_KH = range(7)
    NATURAL = (OP_AT, OP_RT, OP_V, OP_BH, OP_KH)
    STACKED = (OP_AT, OP_BT, OP_KT, OP_V)

    @pl.when(i == 0)
    def _():
        state_ref[...] = jnp.zeros_like(state_ref)

    hblk = hblk_ref[...]

    def head_sum(x):
        xb = x.astype(BF16)
        return jnp.concatenate(
            [_dot(xb[:, q * MXU_DIM:(q + 1) * MXU_DIM], hblk) for q in range(width // MXU_DIM)], axis=1)

    ti = lax.broadcasted_iota(jnp.int32, (CHUNK, PAIR), 0)
    si = lax.broadcasted_iota(jnp.int32, (CHUNK, PAIR), 1) & (CHUNK - 1)
    strict = ti > si
    incl = ti >= si
    eye = (ti == si).astype(F32)

    def level_mask(rows, s):
        cols = lax.broadcasted_iota(jnp.int32, rows.shape, 1) & (CHUNK - 1)
        sh = s.bit_length() - 1
        return ((rows >> (sh + 1)) == (cols >> (sh + 1))) & (((rows >> sh) & 1) == 1) & (((cols >> sh) & 1) == 0)

    levels = []
    s = 1
    while s < CHUNK:
        levels.append((s, level_mask(ti, s) if s < SUBLANES else level_mask(_odd_blocks(ti, s), s)))
        s *= 2
    tri_r = lax.broadcasted_iota(jnp.int32, (span, span), 0)
    tri_c = lax.broadcasted_iota(jnp.int32, (span, span), 1)
    cum_tri = ((tri_r >= tri_c) & ((tri_r >> LOG_CHUNK) == (tri_c >> LOG_CHUNK))).astype(BF16)
    head_masks = {}
    for n in (PAIR, 2 * PAIR):
        first = (lax.broadcasted_iota(jnp.int32, (CHUNK, n), 1) & HEAD_DIM) == 0
        head_masks[n] = (first.astype(BF16), (~first).astype(BF16))
    pr = lax.broadcasted_iota(jnp.int32, (PAIR, 2 * PAIR), 0)
    pc = lax.broadcasted_iota(jnp.int32, (PAIR, 2 * PAIR), 1)
    diag_pp = lax.broadcasted_iota(jnp.int32, (PAIR, PAIR), 0) == lax.broadcasted_iota(jnp.int32, (PAIR, PAIR), 1)
    same_head_pp = (pr >> LOG_CHUNK) == ((pc >> LOG_CHUNK) & 1)
    zeros_st = jnp.zeros((c2, PAIR), BF16)
    zeros_nt = jnp.zeros((CHUNK, PAIR), BF16)

    def stack(xb):
        first, second = head_masks[xb.shape[1]]
        return jnp.concatenate([xb * first, xb * second], axis=0)

    def prepare(rows):
        r = zr_ref[rows, 0:width].astype(F32)
        k = zr_ref[rows, width:2 * width].astype(F32)
        v = zr_ref[rows, 2 * width:3 * width].astype(F32)
        lw = zl_ref[rows, 0:LAT_DECAY].astype(F32)
        la = zl_ref[rows, LAT_DECAY:LAT_DECAY + LAT_AAA]
        lg = zl_ref[rows, LAT_DECAY + LAT_AAA:LAT_COLS].astype(F32)
        yield

        ld = DECAY_SCALE * _sigmoid_t(w0_ref[...] + _dot(jnp.tanh(lw).astype(BF16), w2_ref[...]))
        a = _sigmoid_t(a0_ref[...] + _dot(la, a2_ref[...]))
        gate_s[...] = _dot(_sigmoid_t(lg).astype(BF16), g2_ref[...])
        yield
        kk = k * kk_ref[...]
        kk = kk * lax.rsqrt(jnp.maximum(head_sum(kk * kk), L2_EPS * L2_EPS))
        k2 = k * (1.0 + (a - 1.0) * ka_ref[...])
        bonus_s[...] = head_sum(r * k2 * rk_ref[...]) * v
        bvec = kk * a
        yield

        l1 = ld.astype(BF16)
        l2 = (ld - l1.astype(F32)).astype(BF16)
        gcum = _dot(cum_tri, l1) + _dot(cum_tri, l2)
        gtots = [gcum[(sub + 1) * CHUNK - 1:(sub + 1) * CHUNK, :] for sub in range(SUBCHUNKS)]
        gtot = jnp.concatenate([jnp.broadcast_to(x, (CHUNK, width)) for x in gtots], axis=0)
        e_neg = jnp.exp(-gcum)
        e_end = jnp.exp(gtot - gcum)
        gend_s[...] = jnp.exp(jnp.concatenate(gtots, axis=0))
        full = {OP_AT: -kk * jnp.exp(gcum - ld), OP_RT: r * jnp.exp(gcum), OP_BT: bvec * e_neg, OP_KT: k2 * e_neg,
                OP_V: v, OP_BH: bvec * e_end, OP_KH: k2 * e_end}
        yield
        for op, x in full.items():
            for u, (sub, p) in enumerate(units):
                xb = x[sub * CHUNK:(sub + 1) * CHUNK, p * PAIR:(p + 1) * PAIR].astype(BF16)
                if op in NATURAL:
                    nat_s[NATURAL.index(op), u] = xb
                if op in STACKED:
                    stk_s[STACKED.index(op), u] = stack(xb)
            if op % 2 == 1:
                yield

    def solve(rows):
        pairs = range(len(units))
        at, rt, vn, bh, kh = [[nat_s[o, p] for p in pairs] for o in range(len(NATURAL))]
        at_s, bt_s, kt_s, v_s = [[stk_s[o, p] for p in pairs] for o in range(len(STACKED))]
        g_end = gend_s[...]
        bonus = bonus_s[...]
        gate = gate_s[...]

        scores = [_dot_nt(jnp.concatenate([at[p], rt[p]], axis=0), jnp.concatenate([bt_s[p], kt_s[p]], axis=0)) for p in pairs]
        yield
        a_ab = [jnp.where(strict, sc[0:CHUNK, 0:PAIR], 0.0) for sc in scores]
        a_ak = [jnp.where(strict, sc[0:CHUNK, PAIR:2 * PAIR], 0.0).astype(BF16) for sc in scores]
        a_r = [jnp.concatenate([jnp.where(incl, sc[CHUNK:c2, 0:PAIR], 0.0), jnp.where(incl, sc[CHUNK:c2, PAIR:2 * PAIR], 0.0)],
                               axis=1).astype(BF16) for sc in scores]

        tinv = [eye + jnp.where(levels[0][1], x, 0.0) for x in a_ab]
        yield
        for s, m in levels[1:]:
            tb = [t.astype(BF16) for t in tinv]
            tstk = [stack(t) for t in tb]
            if s < SUBLANES:
                inner = [_dot(jnp.where(m, a_ab[p], 0.0).astype(BF16), tstk[p]).astype(BF16) for p in pairs]
                tinv = [tinv[p] + _dot(tb[p], stack(inner[p])) for p in pairs]
            else:
                inner = [_dot(jnp.where(m, _odd_blocks(a_ab[p], s), 0.0).astype(BF16), tstk[p]) for p in pairs]
                inner = [stack(_spread_odd_blocks(x, s).astype(BF16)) for x in inner]
                upd = [_dot(_odd_blocks(tinv[p], s).astype(BF16), inner[p]) for p in pairs]
                tinv = [tinv[p] + _spread_odd_blocks(upd[p], s) for p in pairs]
            yield
        tb = [t.astype(BF16) for t in tinv]

        akv = [_dot(a_ak[p], v_s[p]).astype(BF16) for p in pairs]
        wub = [_dot(tb[p], jnp.concatenate([at_s[p], stack(akv[p])], axis=1)).astype(BF16) for p in pairs]
        qy = [_dot(a_r[p], jnp.concatenate([stack(wub[p]), jnp.concatenate([zeros_st, v_s[p]], axis=1)], axis=0))
              for p in pairs]
        pz = [_dot_tn(jnp.concatenate([bh[p], kh[p]], axis=0),
                      jnp.concatenate([wub[p], jnp.concatenate([zeros_nt, vn[p]], axis=1)], axis=0)) for p in pairs]
        pz = [jnp.where(same_head_pp, x, 0.0) for x in pz]
        yield
        lhs = []
        for u, (sub, p) in enumerate(units):
            qm = rt[u].astype(F32) + qy[u][:, 0:PAIR]
            p_mat = jnp.where(diag_pp, g_end[sub:sub + 1, p * PAIR:(p + 1) * PAIR], 0.0) + pz[u][:, 0:PAIR]
            lhs.append(jnp.concatenate([qm, p_mat], axis=0).astype(BF16))
        state = [state_ref[p] for p in range(n_pairs)]
        ys = []
        for u, (sub, p) in enumerate(units):
            yv = qy[u][:, PAIR:2 * PAIR]
            both = _dot(lhs[u], state[p].astype(BF16))
            ys.append(both[0:CHUNK, :] + yv)
            state[p] = both[CHUNK:CHUNK + PAIR, :] + pz[u][:, PAIR:2 * PAIR]
        for p in range(n_pairs):
            state_ref[p] = state[p]
        y = jnp.concatenate([jnp.concatenate(ys[sub * n_pairs:(sub + 1) * n_pairs], axis=1) for sub in range(SUBCHUNKS)], axis=0)

        d = y - head_sum(y) * (1.0 / HEAD_DIM)
        var = head_sum(d * d) * (1.0 / HEAD_DIM)
        yn = d * lax.rsqrt(var + GN_EPS) * gnw_ref[...] + gnb_ref[...]
        o_ref[rows, :] = ((yn + bonus) * gate).astype(o_ref.dtype)
        yield

    def chunk_rows(c):
        return pl.ds(pl.multiple_of(c * span, span), span)

    for _ in prepare(pl.ds(0, span)):
        pass

    def chunk_body(c, carry):
        stages = solve(chunk_rows(c))
        next(stages)
        nxt = jnp.minimum(c + 1, n_trips - 1)
        pieces = prepare(chunk_rows(nxt))
        for _ in itertools.zip_longest(pieces, stages):
            pass
        return carry

    lax.fori_loop(0, n_trips, chunk_body, 0)


def _rwkv(p, batch, seq, lat_block, w0, w2p, a0, a2p, g2p, k_k, k_a, r_k, gn_w, gn_b, *, ct=2048):
    width = w0.shape[1]
    n_heads = width // HEAD_DIM
    assert seq % ct == 0 and ct % (SUBCHUNKS * CHUNK) == 0
    steps = seq // ct
    blk = jnp.arange(MXU_DIM, dtype=jnp.int32) // HEAD_DIM
    hblk = (blk[:, None] == blk[None, :]).astype(BF16)
    row = lambda shape: pl.BlockSpec(shape, lambda b, i: (0, 0))
    kernel = functools.partial(_rwkv_kernel, width=width, ct=ct)
    return pl.pallas_call(
        kernel,
        grid=(batch, steps),
        in_specs=[
            pl.BlockSpec((ct, 3 * width), lambda b, i: (b * steps + i, 0)),
            pl.BlockSpec((ct, LAT_COLS), lambda b, i: (b * steps + i, lat_block)),
            row((1, width)), row((LAT_DECAY, width)),
            row((1, width)), row((LAT_AAA, width)),
            row((LAT_GATE, width)),
            row((1, width)), row((1, width)), row((1, width)), row((1, width)), row((1, width)),
            row((MXU_DIM, MXU_DIM)),
        ],
        out_specs=pl.BlockSpec((ct, width), lambda b, i: (b * steps + i, 0)),
        out_shape=jax.ShapeDtypeStruct((batch * seq, width), BF16),
        scratch_shapes=[
            pltpu.VMEM((n_heads // 2, PAIR, PAIR), F32),
            pltpu.VMEM((5, SUBCHUNKS * n_heads // 2, CHUNK, PAIR), BF16),
            pltpu.VMEM((4, SUBCHUNKS * n_heads // 2, 2 * CHUNK, PAIR), BF16),
            pltpu.VMEM((SUBCHUNKS, width), F32), pltpu.VMEM((SUBCHUNKS * CHUNK, width), F32), pltpu.VMEM((SUBCHUNKS * CHUNK, width), F32),
        ],
        compiler_params=_params(("arbitrary", "arbitrary")),
        name="rwkv",
    )(p, p, w0, w2p, a0, a2p, g2p, k_k, k_a, r_k, gn_w, gn_b, hblk)


def _pooled_branch(z, halo, pos, w_ref, scale_ref, group_dim):
    run = jnp.concatenate([halo, z], axis=0)
    span = 1
    outs = []
    for gi, win in enumerate(POOL_WINDOWS):
        cols = slice(gi * group_dim, (gi + 1) * group_dim)
        while span < win:
            run = run + pltpu.roll(run, span, 0)
            span *= 2
        pooled = run[POOL_HALO:, cols] / jnp.minimum(pos, float(win))
        mixed = (pooled - z[:, cols]).astype(BF16)
        outs.append(_dot(mixed, w_ref[gi]) * scale_ref[:, cols])
    return jnp.concatenate(outs, axis=1)


def _mix_kernel(ya_ref, zb_ref, ga_ref, gb_ref, h_ref, pw_ref, ps_ref, wa_ref, wb_ref, wo_ref, gpost_ref, o_ref, halo_ref,
                *, tiles_per_seq):
    i = pl.program_id(0)
    tm = zb_ref.shape[0]
    group_dim = pw_ref.shape[1]
    t0 = (i % tiles_per_seq) * tm

    @pl.when(t0 == 0)
    def _():
        halo_ref[...] = jnp.zeros_like(halo_ref)

    z = zb_ref[...].astype(F32)
    pos = (t0 + lax.broadcasted_iota(jnp.int32, (tm, group_dim), 0) + 1).astype(F32)
    yb = _pooled_branch(z, halo_ref[...], pos, pw_ref, ps_ref, group_dim).astype(BF16)
    halo_ref[...] = z[tm - POOL_HALO:tm, :]

    pa = _dot(ya_ref[...], wa_ref[...])
    pb = _dot(yb, wb_ref[...])
    m = (_sigmoid_t(ga_ref[...].astype(F32)) * pa + _sigmoid_t(gb_ref[...].astype(F32)) * pb).astype(BF16)
    mx = _dot(m, wo_ref[...])
    o_ref[...] = h_ref[...] + _rms(mx, gpost_ref[...])


def _mix_out(ya, p, pool_block, ga_block, gb_block, h, pool_w, pool_scale, wa, wb, wo, g_post, seq, *, tm=512):
    t, d = h.shape
    wa_rows, wb_rows = wa.shape[0], wb.shape[0]
    groups, group_dim, _ = pool_w.shape
    assert seq % tm == 0 and groups * group_dim == wb_rows
    const = lambda shape: pl.BlockSpec(shape, lambda i: (0,) * len(shape), pipeline_mode=pl.Buffered(1))
    kernel = functools.partial(_mix_kernel, tiles_per_seq=seq // tm)
    return pl.pallas_call(
        kernel,
        grid=(t // tm,),
        in_specs=[
            pl.BlockSpec((tm, wa_rows), lambda i: (i, 0)),
            pl.BlockSpec((tm, wb_rows), lambda i: (i, pool_block)),
            pl.BlockSpec((tm, d), lambda i: (i, ga_block)),
            pl.BlockSpec((tm, d), lambda i: (i, gb_block)),
            pl.BlockSpec((tm, d), lambda i: (i, 0)),
            const((groups, group_dim, group_dim)), pl.BlockSpec((1, wb_rows), lambda i: (0, 0)),
            const((wa_rows, d)), const((wb_rows, d)), const((d, d)),
            pl.BlockSpec((1, d), lambda i: (0, 0)),
        ],
        out_specs=pl.BlockSpec((tm, d), lambda i: (i, 0)),
        out_shape=jax.ShapeDtypeStruct((t, d), F32),
        scratch_shapes=[pltpu.VMEM((POOL_HALO, wb_rows), F32)],
        compiler_params=_params(("arbitrary",)),
        name="mix_out",
    )(ya, p, p, p, h, pool_w, pool_scale, wa, wb, wo, g_post)


def _pad_cols(w, n):
    return jnp.pad(w, ((0, 0), (0, n - w.shape[1])))


def _pad_rows(w, n):
    return jnp.pad(w, ((0, n - w.shape[0]), (0, 0)))


def kernel(x, ln_ffn1_pre, ln_ffn1_post, ffn1_gate, ffn1_up, ffn1_down, ln_mix_pre, ln_mix_post, w_in, rwkv_mu, rwkv_w0, rwkv_w2, rwkv_a0, rwkv_a2, rwkv_g2, rwkv_k_k, rwkv_k_a, rwkv_r_k, rwkv_gn_w, rwkv_gn_b, w_proj_a, pool_w, pool_scale, w_proj_b, w_out, ln_ffn2_pre, ln_ffn2_post, ffn2_gate, ffn2_up, ffn2_down):
    batch, seq, d = x.shape
    depth = w_in.shape[0]
    width = rwkv_w0.shape[1]
    pool_width = pool_scale.shape[1]
    n_decay, n_aaa, n_gate = rwkv_w2.shape[1], rwkv_a2.shape[1], rwkv_g2.shape[1]
    rkv = 3 * width
    rwkv_cols = rkv + n_decay + n_aaa + n_gate
    assert n_decay <= LAT_DECAY and n_aaa <= LAT_AAA and n_gate <= LAT_GATE
    assert rkv % pool_width == 0 and (rkv + pool_width) % d == 0 and (rkv + pool_width + 2 * d) % LAT_COLS == 0

    h = x.reshape(batch * seq, d)
    row = lambda v: v.reshape(1, -1)
    for l in range(depth):
        h = _ffn(h, row(ln_ffn1_pre[l]), row(ln_ffn1_post[l]),
                 ffn1_gate[l].astype(BF16), ffn1_up[l].astype(BF16), ffn1_down[l].astype(BF16))

        wl = w_in[l].astype(BF16)
        o1, o2 = rkv + n_decay, rkv + n_decay + n_aaa
        w_cat = jnp.concatenate([
            wl[:, :rkv], wl[:, rwkv_cols:],
            _pad_cols(wl[:, rkv:o1], LAT_DECAY), _pad_cols(wl[:, o1:o2], LAT_AAA), _pad_cols(wl[:, o2:rwkv_cols], LAT_GATE),
        ], axis=1)
        mu = rwkv_mu[l]
        mu_cat = jnp.concatenate([
            mu[:rkv], jnp.zeros((w_in.shape[2] - rwkv_cols,), mu.dtype),
            jnp.pad(mu[rkv:o1], (0, LAT_DECAY - n_decay)), jnp.pad(mu[o1:o2], (0, LAT_AAA - n_aaa)),
            jnp.pad(mu[o2:], (0, LAT_GATE - n_gate))])
        p = _inproj(h, row(ln_mix_pre[l]), w_cat, row(mu_cat), seq)

        pool_block = rkv // pool_width
        ga_block = (rkv + pool_width) // d
        lat_block = (rkv + pool_width + 2 * d) // LAT_COLS
        ya = _rwkv(p, batch, seq, lat_block,
                   row(rwkv_w0[l]), _pad_rows(rwkv_w2[l], LAT_DECAY).astype(BF16),
                   row(rwkv_a0[l]), _pad_rows(rwkv_a2[l], LAT_AAA).astype(BF16),
                   _pad_rows(rwkv_g2[l], LAT_GATE).astype(BF16),
                   row(rwkv_k_k[l]), row(rwkv_k_a[l]), row(rwkv_r_k[l]), row(rwkv_gn_w[l]), row(rwkv_gn_b[l]))
        h = _mix_out(ya, p, pool_block, ga_block, ga_block + 1, h, pool_w[l].astype(BF16), row(pool_scale[l]),
                     w_proj_a[l].astype(BF16), w_proj_b[l].astype(BF16), w_out[l].astype(BF16), row(ln_mix_post[l]), seq)

        h = _ffn(h, row(ln_ffn2_pre[l]), row(ln_ffn2_post[l]),
                 ffn2_gate[l].astype(BF16), ffn2_up[l].astype(BF16), ffn2_down[l].astype(BF16))
    return h.reshape(batch, seq, d)
```

```python
import functools
import itertools
import math

import jax
import jax.numpy as jnp
from jax import lax
from jax.experimental import pallas as pl
from jax.experimental.pallas import tpu as pltpu

F32 = jnp.float32
BF16 = jnp.bfloat16

SUBLANES = 8
MXU_DIM = 256
VMEM_BYTES = 64 * 1024 * 1024
VMEM_LIMIT_BYTES = 56 * 1024 * 1024

NORM_EPS = 1e-6
GN_EPS = 64e-5
L2_EPS = 1e-12
DECAY_SCALE = -math.exp(-0.5)
MACARON_WEIGHT = 0.5

HEAD_DIM = 64
POOL_WINDOWS = (2, 4, 8, 16)
POOL_HALO = max(POOL_WINDOWS)
CHUNK = 64
LOG_CHUNK = CHUNK.bit_length() - 1
ROW_SPLIT = 2
SUBCHUNKS = 2
PAIR = 2 * HEAD_DIM
LAT_DECAY, LAT_AAA, LAT_GATE = 128, 128, 256
LAT_COLS = LAT_DECAY + LAT_AAA + LAT_GATE


def _dot(a, b):
    return jnp.dot(a, b, preferred_element_type=F32)


def _dot_nt(a, b):
    return lax.dot_general(a, b, (((1,), (1,)), ((), ())), preferred_element_type=F32)


def _dot_tn(a, b):
    return lax.dot_general(a, b, (((0,), (0,)), ((), ())), preferred_element_type=F32)


def _rms(x, gain):
    ms = jnp.mean(x * x, axis=-1, keepdims=True)
    return x * lax.rsqrt(ms + NORM_EPS) * gain


def _sigmoid(x):
    return 0.5 * jnp.tanh(0.5 * x) + 0.5


def _params(semantics, vmem_limit_bytes=VMEM_LIMIT_BYTES):
    return pltpu.CompilerParams(dimension_semantics=semantics, vmem_limit_bytes=vmem_limit_bytes)


def _ffn_kernel(x_ref, gpre_ref, gpost_ref, wg_ref, wu_ref, wd_ref, o_ref, xn_ref, *, last_cols):
    j = pl.program_id(1)
    last = pl.num_programs(1) - 1
    tm = x_ref.shape[0]
    halves = [pl.ds(r * (tm // ROW_SPLIT), tm // ROW_SPLIT) for r in range(ROW_SPLIT)]

    def accumulate(cols, rows, first=False):
        xn = xn_ref[rows, :]
        gate = _dot(xn, wg_ref[:, 0:cols])
        up = _dot(xn, wu_ref[:, 0:cols])
        act = (gate * _sigmoid(gate) * up).astype(BF16)
        down = _dot(act, wd_ref[0:cols, :])
        o_ref[rows, :] = down if first else o_ref[rows, :] + down

    @pl.when(j == 0)
    def _():
        for rows in halves:
            xn_ref[rows, :] = _rms(x_ref[rows, :], gpre_ref[...]).astype(BF16)
            accumulate(wg_ref.shape[1], rows, first=True)

    @pl.when((j > 0) & (j < last))
    def _():
        accumulate(wg_ref.shape[1], pl.ds(0, tm))

    @pl.when(j == last)
    def _():
        for rows in halves:
            accumulate(last_cols, rows)
            o_ref[rows, :] = x_ref[rows, :] + MACARON_WEIGHT * _rms(o_ref[rows, :], gpost_ref[...])


def _ffn(x, g_pre, g_post, w_gate, w_up, w_down, *, tm=1024, tf=512):
    t, d = x.shape
    f = w_gate.shape[1]
    vmem = 2 * 2 * tm * d * 4 + tm * d * 2 + 2 * 3 * d * tf * 2 + 3 * tm * tf * 4
    assert vmem <= VMEM_BYTES - (2 << 20)
    steps = pl.cdiv(f, tf)
    kernel = functools.partial(_ffn_kernel, last_cols=f - (steps - 1) * tf)
    return pl.pallas_call(
        kernel,
        grid=(t // tm, steps),
        in_specs=[
            pl.BlockSpec((tm, d), lambda i, j: (i, 0)),
            pl.BlockSpec((1, d), lambda i, j: (0, 0)),
            pl.BlockSpec((1, d), lambda i, j: (0, 0)),
            pl.BlockSpec((d, tf), lambda i, j: (0, j)),
            pl.BlockSpec((d, tf), lambda i, j: (0, j)),
            pl.BlockSpec((tf, d), lambda i, j: (j, 0)),
        ],
        out_specs=pl.BlockSpec((tm, d), lambda i, j: (i, 0)),
        out_shape=jax.ShapeDtypeStruct((t, d), F32),
        scratch_shapes=[pltpu.VMEM((tm, d), BF16)],
        compiler_params=_params(("parallel", "arbitrary"), vmem),
        name="ffn",
    )(x, g_pre, g_post, w_gate, w_up, w_down)


def _token_shift_lerp(z, prev_row, mu):
    rolled = pltpu.roll(z, 1, 0)
    row = lax.broadcasted_iota(jnp.int32, z.shape, 0)
    shifted = jnp.where(row == 0, prev_row, rolled)
    return z + (shifted - z) * mu


def _inproj_kernel(h_ref, g_ref, w_ref, mu_ref, o_ref, u_ref, prev_ref, *, tiles_per_seq, first_cols):
    i = pl.program_id(0)
    j = pl.program_id(1)
    tm = h_ref.shape[0]

    def finish(z, cols):
        first_of_seq = (i % tiles_per_seq) == 0
        prev = jnp.where(first_of_seq, 0.0, prev_ref[j, :, 0:cols])
        prev_ref[j, :, 0:cols] = z[tm - 1:tm, :]
        o_ref[:, 0:cols] = _token_shift_lerp(z, prev, mu_ref[:, 0:cols]).astype(o_ref.dtype)

    @pl.when(j == 0)
    def _():
        parts = []
        for r in range(ROW_SPLIT):
            rows = pl.ds(r * (tm // ROW_SPLIT), tm // ROW_SPLIT)
            u_ref[rows, :] = _rms(h_ref[rows, :], g_ref[...]).astype(BF16)
            parts.append(_dot(u_ref[rows, :], w_ref[:, 0:first_cols]))
        finish(jnp.concatenate(parts, axis=0), first_cols)

    @pl.when(j > 0)
    def _():
        finish(_dot(u_ref[...], w_ref[...]), w_ref.shape[1])


def _inproj(h, gain, w, mu, seq, *, tm=1024, tn=1024):
    t, d = h.shape
    n = w.shape[1]
    assert seq % tm == 0
    steps = pl.cdiv(n, tn)
    kernel = functools.partial(_inproj_kernel, tiles_per_seq=seq // tm, first_cols=n - (steps - 1) * tn)
    block = lambda j: (j + steps - 1) % steps
    return pl.pallas_call(
        kernel,
        grid=(t // tm, steps),
        in_specs=[
            pl.BlockSpec((tm, d), lambda i, j: (i, 0)),
            pl.BlockSpec((1, d), lambda i, j: (0, 0)),
            pl.BlockSpec((d, tn), lambda i, j: (0, block(j))),
            pl.BlockSpec((1, tn), lambda i, j: (0, block(j))),
        ],
        out_specs=pl.BlockSpec((tm, tn), lambda i, j: (i, block(j))),
        out_shape=jax.ShapeDtypeStruct((t, n), BF16),
        scratch_shapes=[pltpu.VMEM((tm, d), BF16), pltpu.VMEM((steps, 1, tn), F32)],
        compiler_params=_params(("arbitrary", "arbitrary")),
        name="inproj",
    )(h, gain, w, mu)


def _odd_blocks(x, s):
    return jnp.concatenate([x[b * s:(b + 1) * s] for b in range(1, x.shape[0] // s, 2)], axis=0)


def _spread_odd_blocks(u, s):
    zero = jnp.zeros((s, u.shape[1]), u.dtype)
    parts = []
    for b in range(u.shape[0] // s):
        parts += [zero, u[b * s:(b + 1) * s]]
    return jnp.concatenate(parts, axis=0)


def _rwkv_kernel(zr_ref, zl_ref, w0_ref, w2_ref, a0_ref, a2_ref, g2_ref,
                 kk_ref, ka_ref, rk_ref, gnw_ref, gnb_ref, hblk_ref,
                 o_ref,
                 state_ref, nat_s, stk_s, gend_s, bonus_s, gate_s, *, width, ct):
    i = pl.program_id(1)
    n_pairs = width // PAIR
    span = SUBCHUNKS * CHUNK
    n_trips = ct // span
    units = [(sub, p) for sub in range(SUBCHUNKS) for p in range(n_pairs)]
    c2 = 2 * CHUNK
    OP_AT, OP_RT, OP_BT, OP_KT, OP_V, OP_BH, OP_KH = range(7)
    NATURAL = (OP_AT, OP_RT, OP_V, OP_BH, OP_KH)
    STACKED = (OP_AT, OP_BT, OP_KT, OP_V)

    @pl.when(i == 0)
    def _():
        state_ref[...] = jnp.zeros_like(state_ref)

    hblk = hblk_ref[...]

    def head_sum(x):
        xb = x.astype(BF16)
        return jnp.concatenate(
            [_dot(xb[:, q * MXU_DIM:(q + 1) * MXU_DIM], hblk) for q in range(width // MXU_DIM)], axis=1)

    ti = lax.broadcasted_iota(jnp.int32, (CHUNK, PAIR), 0)
    si = lax.broadcasted_iota(jnp.int32, (CHUNK, PAIR), 1) & (CHUNK - 1)
    strict = ti > si
    incl = ti >= si
    eye = (ti == si).astype(F32)

    def level_mask(rows, s):
        cols = lax.broadcasted_iota(jnp.int32, rows.shape, 1) & (CHUNK - 1)
        sh = s.bit_length() - 1
        return ((rows >> (sh + 1)) == (cols >> (sh + 1))) & (((rows >> sh) & 1) == 1) & (((cols >> sh) & 1) == 0)

    levels = []
    s = 1
    while s < CHUNK:
        levels.append((s, level_mask(ti, s) if s < SUBLANES else level_mask(_odd_blocks(ti, s), s)))
        s *= 2
    tri_r = lax.broadcasted_iota(jnp.int32, (span, span), 0)
    tri_c = lax.broadcasted_iota(jnp.int32, (span, span), 1)
    cum_tri = ((tri_r >= tri_c) & ((tri_r >> LOG_CHUNK) == (tri_c >> LOG_CHUNK))).astype(BF16)
    head_masks = {}
    for n in (PAIR, 2 * PAIR):
        first = (lax.broadcasted_iota(jnp.int32, (CHUNK, n), 1) & HEAD_DIM) == 0
        head_masks[n] = (first.astype(BF16), (~first).astype(BF16))
    pr = lax.broadcasted_iota(jnp.int32, (PAIR, 2 * PAIR), 0)
    pc = lax.broadcasted_iota(jnp.int32, (PAIR, 2 * PAIR), 1)
    diag_pp = lax.broadcasted_iota(jnp.int32, (PAIR, PAIR), 0) == lax.broadcasted_iota(jnp.int32, (PAIR, PAIR), 1)
    same_head_pp = (pr >> LOG_CHUNK) == ((pc >> LOG_CHUNK) & 1)
    zeros_st = jnp.zeros((c2, PAIR), BF16)
    zeros_nt = jnp.zeros((CHUNK, PAIR), BF16)

    def stack(xb):
        first, second = head_masks[xb.shape[1]]
        return jnp.concatenate([xb * first, xb * second], axis=0)

    def prepare(rows):
        r = zr_ref[rows, 0:width].astype(F32)
        k = zr_ref[rows, width:2 * width].astype(F32)
        v = zr_ref[rows, 2 * width:3 * width].astype(F32)
        lw = zl_ref[rows, 0:LAT_DECAY].astype(F32)
        la = zl_ref[rows, LAT_DECAY:LAT_DECAY + LAT_AAA]
        lg = zl_ref[rows, LAT_DECAY + LAT_AAA:LAT_COLS].astype(F32)
        yield

        ld = DECAY_SCALE * _sigmoid(w0_ref[...] + _dot(jnp.tanh(lw).astype(BF16), w2_ref[...]))
        a = _sigmoid(a0_ref[...] + _dot(la, a2_ref[...]))
        gate_s[...] = _dot(_sigmoid(lg).astype(BF16), g2_ref[...])
        yield
        kk = k * kk_ref[...]
        kk = kk * lax.rsqrt(jnp.maximum(head_sum(kk * kk), L2_EPS * L2_EPS))
        k2 = k * (1.0 + (a - 1.0) * ka_ref[...])
        bonus_s[...] = head_sum(r * k2 * rk_ref[...]) * v
        bvec = kk * a
        yield

        l1 = ld.astype(BF16)
        l2 = (ld - l1.astype(F32)).astype(BF16)
        gcum = _dot(cum_tri, l1) + _dot(cum_tri, l2)
        gtots = [gcum[(sub + 1) * CHUNK - 1:(sub + 1) * CHUNK, :] for sub in range(SUBCHUNKS)]
        gtot = jnp.concatenate([jnp.broadcast_to(x, (CHUNK, width)) for x in gtots], axis=0)
        e_neg = jnp.exp(-gcum)
        e_end = jnp.exp(gtot - gcum)
        gend_s[...] = jnp.exp(jnp.concatenate(gtots, axis=0))
        full = {OP_AT: -kk * jnp.exp(gcum - ld), OP_RT: r * jnp.exp(gcum), OP_BT: bvec * e_neg, OP_KT: k2 * e_neg,
                OP_V: v, OP_BH: bvec * e_end, OP_KH: k2 * e_end}
        yield
        for op, x in full.items():
            for u, (sub, p) in enumerate(units):
                xb = x[sub * CHUNK:(sub + 1) * CHUNK, p * PAIR:(p + 1) * PAIR].astype(BF16)
                if op in NATURAL:
                    nat_s[NATURAL.index(op), u] = xb
                if op in STACKED:
                    stk_s[STACKED.index(op), u] = stack(xb)
            if op % 2 == 1:
                yield

    def solve(rows):
        pairs = range(len(units))
        at, rt, vn, bh, kh = [[nat_s[o, p] for p in pairs] for o in range(len(NATURAL))]
        at_s, bt_s, kt_s, v_s = [[stk_s[o, p] for p in pairs] for o in range(len(STACKED))]
        g_end = gend_s[...]
        bonus = bonus_s[...]
        gate = gate_s[...]

        scores = [_dot_nt(jnp.concatenate([at[p], rt[p]], axis=0), jnp.concatenate([bt_s[p], kt_s[p]], axis=0)) for p in pairs]
        yield
        a_ab = [jnp.where(strict, sc[0:CHUNK, 0:PAIR], 0.0) for sc in scores]
        a_ak = [jnp.where(strict, sc[0:CHUNK, PAIR:2 * PAIR], 0.0).astype(BF16) for sc in scores]
        a_r = [jnp.concatenate([jnp.where(incl, sc[CHUNK:c2, 0:PAIR], 0.0), jnp.where(incl, sc[CHUNK:c2, PAIR:2 * PAIR], 0.0)],
                               axis=1).astype(BF16) for sc in scores]

        tinv = [eye + jnp.where(levels[0][1], x, 0.0) for x in a_ab]
        yield
        for s, m in levels[1:]:
            tb = [t.astype(BF16) for t in tinv]
            tstk = [stack(t) for t in tb]
            if s < SUBLANES:
                inner = [_dot(jnp.where(m, a_ab[p], 0.0).astype(BF16), tstk[p]).astype(BF16) for p in pairs]
                tinv = [tinv[p] + _dot(tb[p], stack(inner[p])) for p in pairs]
            else:
                inner = [_dot(jnp.where(m, _odd_blocks(a_ab[p], s), 0.0).astype(BF16), tstk[p]) for p in pairs]
                inner = [stack(_spread_odd_blocks(x, s).astype(BF16)) for x in inner]
                upd = [_dot(_odd_blocks(tinv[p], s).astype(BF16), inner[p]) for p in pairs]
                tinv = [tinv[p] + _spread_odd_blocks(upd[p], s) for p in pairs]
            yield
        tb = [t.astype(BF16) for t in tinv]

        akv = [_dot(a_ak[p], v_s[p]).astype(BF16) for p in pairs]
        wub = [_dot(tb[p], jnp.concatenate([at_s[p], stack(akv[p])], axis=1)).astype(BF16) for p in pairs]
        qy = [_dot(a_r[p], jnp.concatenate([stack(wub[p]), jnp.concatenate([zeros_st, v_s[p]], axis=1)], axis=0))
              for p in pairs]
        pz = [_dot_tn(jnp.concatenate([bh[p], kh[p]], axis=0),
                      jnp.concatenate([wub[p], jnp.concatenate([zeros_nt, vn[p]], axis=1)], axis=0)) for p in pairs]
        pz = [jnp.where(same_head_pp, x, 0.0) for x in pz]
        yield
        lhs = []
        for u, (sub, p) in enumerate(units):
            qm = rt[u].astype(F32) + qy[u][:, 0:PAIR]
            p_mat = jnp.where(diag_pp, g_end[sub:sub + 1, p * PAIR:(p + 1) * PAIR], 0.0) + pz[u][:, 0:PAIR]
            lhs.append(jnp.concatenate([qm, p_mat], axis=0).astype(BF16))
        state = [state_ref[p] for p in range(n_pairs)]
        ys = []
        for u, (sub, p) in enumerate(units):
            yv = qy[u][:, PAIR:2 * PAIR]
            both = _dot(lhs[u], state[p].astype(BF16))
            ys.append(both[0:CHUNK, :] + yv)
            state[p] = both[CHUNK:CHUNK + PAIR, :] + pz[u][:, PAIR:2 * PAIR]
        for p in range(n_pairs):
            state_ref[p] = state[p]
        y = jnp.concatenate([jnp.concatenate(ys[sub * n_pairs:(sub + 1) * n_pairs], axis=1) for sub in range(SUBCHUNKS)], axis=0)

        d = y - head_sum(y) * (1.0 / HEAD_DIM)
        var = head_sum(d * d) * (1.0 / HEAD_DIM)
        yn = d * lax.rsqrt(var + GN_EPS) * gnw_ref[...] + gnb_ref[...]
        o_ref[rows, :] = ((yn + bonus) * gate).astype(o_ref.dtype)
        yield

    def chunk_rows(c):
        return pl.ds(pl.multiple_of(c * span, span), span)

    for _ in prepare(pl.ds(0, span)):
        pass

    def chunk_body(c, carry):
        stages = solve(chunk_rows(c))
        next(stages)
        nxt = jnp.minimum(c + 1, n_trips - 1)
        pieces = prepare(chunk_rows(nxt))
        for _ in itertools.zip_longest(pieces, stages):
            pass
        return carry

    lax.fori_loop(0, n_trips, chunk_body, 0)


def _rwkv(p, batch, seq, lat_block, w0, w2p, a0, a2p, g2p, k_k, k_a, r_k, gn_w, gn_b, *, ct=2048):
    width = w0.shape[1]
    n_heads = width // HEAD_DIM
    assert seq % ct == 0 and ct % (SUBCHUNKS * CHUNK) == 0
    steps = seq // ct
    blk = jnp.arange(MXU_DIM, dtype=jnp.int32) // HEAD_DIM
    hblk = (blk[:, None] == blk[None, :]).astype(BF16)
    row = lambda shape: pl.BlockSpec(shape, lambda b, i: (0, 0))
    kernel = functools.partial(_rwkv_kernel, width=width, ct=ct)
    return pl.pallas_call(
        kernel,
        grid=(batch, steps),
        in_specs=[
            pl.BlockSpec((ct, 3 * width), lambda b, i: (b * steps + i, 0)),
            pl.BlockSpec((ct, LAT_COLS), lambda b, i: (b * steps + i, lat_block)),
            row((1, width)), row((LAT_DECAY, width)),
            row((1, width)), row((LAT_AAA, width)),
            row((LAT_GATE, width)),
            row((1, width)), row((1, width)), row((1, width)), row((1, width)), row((1, width)),
            row((MXU_DIM, MXU_DIM)),
        ],
        out_specs=pl.BlockSpec((ct, width), lambda b, i: (b * steps + i, 0)),
        out_shape=jax.ShapeDtypeStruct((batch * seq, width), BF16),
        scratch_shapes=[
            pltpu.VMEM((n_heads // 2, PAIR, PAIR), F32),
            pltpu.VMEM((5, SUBCHUNKS * n_heads // 2, CHUNK, PAIR), BF16),
            pltpu.VMEM((4, SUBCHUNKS * n_heads // 2, 2 * CHUNK, PAIR), BF16),
            pltpu.VMEM((SUBCHUNKS, width), F32), pltpu.VMEM((SUBCHUNKS * CHUNK, width), F32), pltpu.VMEM((SUBCHUNKS * CHUNK, width), F32),
        ],
        compiler_params=_params(("arbitrary", "arbitrary")),
        name="rwkv",
    )(p, p, w0, w2p, a0, a2p, g2p, k_k, k_a, r_k, gn_w, gn_b, hblk)


def _pooled_branch(z, halo, pos, w_ref, scale_ref, group_dim):
    run = jnp.concatenate([halo, z], axis=0)
    span = 1
    outs = []
    for gi, win in enumerate(POOL_WINDOWS):
        cols = slice(gi * group_dim, (gi + 1) * group_dim)
        while span < win:
            run = run + pltpu.roll(run, span, 0)
            span *= 2
        pooled = run[POOL_HALO:, cols] / jnp.minimum(pos, float(win))
        mixed = (pooled - z[:, cols]).astype(BF16)
        outs.append(_dot(mixed, w_ref[gi]) * scale_ref[:, cols])
    return jnp.concatenate(outs, axis=1)


def _mix_kernel(ya_ref, zb_ref, ga_ref, gb_ref, h_ref, pw_ref, ps_ref, wa_ref, wb_ref, wo_ref, gpost_ref, o_ref, halo_ref,
                *, tiles_per_seq):
    i = pl.program_id(0)
    tm = zb_ref.shape[0]
    group_dim = pw_ref.shape[1]
    t0 = (i % tiles_per_seq) * tm

    @pl.when(t0 == 0)
    def _():
        halo_ref[...] = jnp.zeros_like(halo_ref)

    z = zb_ref[...].astype(F32)
    pos = (t0 + lax.broadcasted_iota(jnp.int32, (tm, group_dim), 0) + 1).astype(F32)
    yb = _pooled_branch(z, halo_ref[...], pos, pw_ref, ps_ref, group_dim).astype(BF16)
    halo_ref[...] = z[tm - POOL_HALO:tm, :]

    pa = _dot(ya_ref[...], wa_ref[...])
    pb = _dot(yb, wb_ref[...])
    m = (_sigmoid(ga_ref[...].astype(F32)) * pa + _sigmoid(gb_ref[...].astype(F32)) * pb).astype(BF16)
    mx = _dot(m, wo_ref[...])
    o_ref[...] = h_ref[...] + _rms(mx, gpost_ref[...])


def _mix_out(ya, p, pool_block, ga_block, gb_block, h, pool_w, pool_scale, wa, wb, wo, g_post, seq, *, tm=512):
    t, d = h.shape
    wa_rows, wb_rows = wa.shape[0], wb.shape[0]
    groups, group_dim, _ = pool_w.shape
    assert seq % tm == 0 and groups * group_dim == wb_rows
    const = lambda shape: pl.BlockSpec(shape, lambda i: (0,) * len(shape), pipeline_mode=pl.Buffered(1))
    kernel = functools.partial(_mix_kernel, tiles_per_seq=seq // tm)
    return pl.pallas_call(
        kernel,
        grid=(t // tm,),
        in_specs=[
            pl.BlockSpec((tm, wa_rows), lambda i: (i, 0)),
            pl.BlockSpec((tm, wb_rows), lambda i: (i, pool_block)),
            pl.BlockSpec((tm, d), lambda i: (i, ga_block)),
            pl.BlockSpec((tm, d), lambda i: (i, gb_block)),
            pl.BlockSpec((tm, d), lambda i: (i, 0)),
            const((groups, group_dim, group_dim)), pl.BlockSpec((1, wb_rows), lambda i: (0, 0)),
            const((wa_rows, d)), const((wb_rows, d)), const((d, d)),
            pl.BlockSpec((1, d), lambda i: (0, 0)),
        ],
        out_specs=pl.BlockSpec((tm, d), lambda i: (i, 0)),
        out_shape=jax.ShapeDtypeStruct((t, d), F32),
        scratch_shapes=[pltpu.VMEM((POOL_HALO, wb_rows), F32)],
        compiler_params=_params(("arbitrary",)),
        name="mix_out",
    )(ya, p, p, p, h, pool_w, pool_scale, wa, wb, wo, g_post)


def _pad_cols(w, n):
    return jnp.pad(w, ((0, 0), (0, n - w.shape[1])))


def _pad_rows(w, n):
    return jnp.pad(w, ((0, n - w.shape[0]), (0, 0)))


def kernel(x, ln_ffn1_pre, ln_ffn1_post, ffn1_gate, ffn1_up, ffn1_down, ln_mix_pre, ln_mix_post, w_in, rwkv_mu, rwkv_w0, rwkv_w2, rwkv_a0, rwkv_a2, rwkv_g2, rwkv_k_k, rwkv_k_a, rwkv_r_k, rwkv_gn_w, rwkv_gn_b, w_proj_a, pool_w, pool_scale, w_proj_b, w_out, ln_ffn2_pre, ln_ffn2_post, ffn2_gate, ffn2_up, ffn2_down):
    batch, seq, d = x.shape
    depth = w_in.shape[0]
    width = rwkv_w0.shape[1]
    pool_width = pool_scale.shape[1]
    n_decay, n_aaa, n_gate = rwkv_w2.shape[1], rwkv_a2.shape[1], rwkv_g2.shape[1]
    rkv = 3 * width
    rwkv_cols = rkv + n_decay + n_aaa + n_gate
    assert n_decay <= LAT_DECAY and n_aaa <= LAT_AAA and n_gate <= LAT_GATE
    assert rkv % pool_width == 0 and (rkv + pool_width) % d == 0 and (rkv + pool_width + 2 * d) % LAT_COLS == 0

    h = x.reshape(batch * seq, d)
    row = lambda v: v.reshape(1, -1)
    for l in range(depth):
        h = _ffn(h, row(ln_ffn1_pre[l]), row(ln_ffn1_post[l]),
                 ffn1_gate[l].astype(BF16), ffn1_up[l].astype(BF16), ffn1_down[l].astype(BF16))

        wl = w_in[l].astype(BF16)
        o1, o2 = rkv + n_decay, rkv + n_decay + n_aaa
        w_cat = jnp.concatenate([
            wl[:, :rkv], wl[:, rwkv_cols:],
            _pad_cols(wl[:, rkv:o1], LAT_DECAY), _pad_cols(wl[:, o1:o2], LAT_AAA), _pad_cols(wl[:, o2:rwkv_cols], LAT_GATE),
        ], axis=1)
        mu = rwkv_mu[l]
        mu_cat = jnp.concatenate([
            mu[:rkv], jnp.zeros((w_in.shape[2] - rwkv_cols,), mu.dtype),
            jnp.pad(mu[rkv:o1], (0, LAT_DECAY - n_decay)), jnp.pad(mu[o1:o2], (0, LAT_AAA - n_aaa)),
            jnp.pad(mu[o2:], (0, LAT_GATE - n_gate))])
        p = _inproj(h, row(ln_mix_pre[l]), w_cat, row(mu_cat), seq)

        pool_block = rkv // pool_width
        ga_block = (rkv + pool_width) // d
        lat_block = (rkv + pool_width + 2 * d) // LAT_COLS
        ya = _rwkv(p, batch, seq, lat_block,
                   row(rwkv_w0[l]), _pad_rows(rwkv_w2[l], LAT_DECAY).astype(BF16),
                   row(rwkv_a0[l]), _pad_rows(rwkv_a2[l], LAT_AAA).astype(BF16),
                   _pad_rows(rwkv_g2[l], LAT_GATE).astype(BF16),
                   row(rwkv_k_k[l]), row(rwkv_k_a[l]), row(rwkv_r_k[l]), row(rwkv_gn_w[l]), row(rwkv_gn_b[l]))
        h = _mix_out(ya, p, pool_block, ga_block, ga_block + 1, h, pool_w[l].astype(BF16), row(pool_scale[l]),
                     w_proj_a[l].astype(BF16), w_proj_b[l].astype(BF16), w_out[l].astype(BF16), row(ln_mix_post[l]), seq)

        h = _ffn(h, row(ln_ffn2_pre[l]), row(ln_ffn2_post[l]),
                 ffn2_gate[l].astype(BF16), ffn2_up[l].astype(BF16), ffn2_down[l].astype(BF16))
    return h.reshape(batch, seq, d)
```
